```python
import math
import jax
import jax.numpy as jnp
from jax import lax
import numpy as np

D_MODEL = 2048
BATCH = 8
SEQ = 2048
DEPTH = 1

ALPHA = (2.0 * DEPTH) ** 0.25
BETA = (8.0 * DEPTH) ** -0.25
LN_EPS = 1e-5
RMS_EPS = 1e-6
HG_DK = 128
HG_DV = 128
HG_HEADS = D_MODEL // 128
HG_WIDTH = HG_HEADS * HG_DK
HG_CHUNK = 64
AT_DH = 128
AT_HEADS = D_MODEL // 128
AT_WIDTH = AT_HEADS * AT_DH
KV_RANK = D_MODEL // 8
IDX_HEADS = 16
IDX_DIM = 128
TOPK_MAX = 256
Q_BLOCK = 128
REL_BUCKETS = 32
REL_MAX_DIST = 128
MEM_LEN = 256
MEM_HEADS = 4
MEM_DH = D_MODEL // MEM_HEADS
PEER_HEADS = 8
PEER_NKEYS = 128
PEER_N = PEER_NKEYS * PEER_NKEYS
PEER_DKEY = 256
PEER_TOPK = 16
PEER_BLOCK = 128
IN_SIZES = (HG_WIDTH, HG_WIDTH, HG_WIDTH, HG_WIDTH, AT_WIDTH, KV_RANK, IDX_HEADS * IDX_DIM, IDX_DIM, IDX_HEADS, D_MODEL, D_MODEL)
IN_COLS = sum(IN_SIZES)

kernel_name = 'hgrn2_dsa_peer_gated_hybrid'


def layer_norm(x, g, b):
    xf = x.astype(jnp.float32)
    mu = jnp.mean(xf, axis=-1, keepdims=True)
    var = jnp.mean(jnp.square(xf - mu), axis=-1, keepdims=True)
    return ((xf - mu) * lax.rsqrt(var + LN_EPS) * g + b).astype(x.dtype)


def rms_norm(x, g):
    xf = x.astype(jnp.float32)
    return (xf * lax.rsqrt(jnp.mean(xf * xf, axis=-1, keepdims=True) + RMS_EPS) * g).astype(x.dtype)


def split_columns(t, sizes):
    idx, acc = [], 0
    for s in sizes[:-1]:
        acc += s
        idx.append(acc)
    return jnp.split(t, idx, axis=-1)


def t5_bucket(dist):
    max_exact = REL_BUCKETS // 2
    d = jnp.maximum(dist, 0)
    large = max_exact + (jnp.log(jnp.maximum(d, 1).astype(jnp.float32) / max_exact)
                         / math.log(REL_MAX_DIST / max_exact) * (REL_BUCKETS - max_exact)).astype(jnp.int32)
    large = jnp.minimum(large, REL_BUCKETS - 1)
    return jnp.where(d < max_exact, d, large)


def hgrn2_branch(q, f_raw, i, g_out, lb, norm_g):
    B, S, _ = q.shape
    C = HG_CHUNK
    N = S // C
    f = lb + (1.0 - lb) * jax.nn.sigmoid(f_raw.astype(jnp.float32))
    log_f = jnp.log(f)
    k = 1.0 - f

    def heads(t, d):
        return t.astype(jnp.float32).reshape(B, N, C, HG_HEADS, d).transpose(0, 3, 1, 2, 4)

    qh, kh, vh, gl = heads(q, HG_DK), heads(k, HG_DK), heads(i, HG_DV), heads(log_f, HG_DK)
    G = jnp.cumsum(gl, axis=3)
    G_last = G[:, :, :, -1:, :]
    G_ref = G[:, :, :, C // 2 - 1:C // 2, :]
    a = jnp.einsum('bhnik,bhnjk->bhnij', qh * jnp.exp(G - G_ref), kh * jnp.exp(G_ref - G))
    a = jnp.where(jnp.tril(jnp.ones((C, C), dtype=bool)), a, 0.0)
    o_intra = jnp.einsum('bhnij,bhnjv->bhniv', a, vh)
    q_dec = qh * jnp.exp(G)
    k_dec = kh * jnp.exp(G_last - G)
    chunk_decay = jnp.exp(G_last[:, :, :, 0, :])

    def step(state, xs):
        qd, kd, vc, dec = xs
        o = jnp.einsum('bhik,bhkv->bhiv', qd, state)
        state = dec[..., None] * state + jnp.einsum('bhjk,bhjv->bhkv', kd, vc)
        return state, o

    s0 = jnp.zeros((B, HG_HEADS, HG_DK, HG_DV), jnp.float32)
    mv = lambda t: jnp.moveaxis(t, 2, 0)
    _, o_inter = lax.scan(step, s0, (mv(q_dec), mv(k_dec), mv(vh), mv(chunk_decay)))
    o = o_intra + jnp.moveaxis(o_inter, 0, 2)
    o = o.transpose(0, 2, 3, 1, 4).reshape(B, S, HG_HEADS, HG_DV)
    o = rms_norm(o, norm_g.reshape(HG_HEADS, HG_DV).astype(jnp.float32))
    o = o.reshape(B, S, HG_WIDTH) * jax.nn.silu(g_out.astype(jnp.float32))
    return o.astype(q.dtype)


def dsa_branch(q, ckv, iq, ik, iw, kv_norm_g, ik_norm_g, ik_norm_b, w_uk, w_uv, rel_bias):
    B, S, _ = q.shape
    K = min(TOPK_MAX, S // 4)
    nb = S // Q_BLOCK
    ckv = rms_norm(ckv, kv_norm_g)
    ik = layer_norm(ik, ik_norm_g, ik_norm_b)
    q = q.reshape(B, S, AT_HEADS, AT_DH)
    q_lat = jnp.einsum('bshd,hrd->bshr', q, w_uk) * AT_DH ** -0.5
    iq = iq.reshape(B, S, IDX_HEADS, IDX_DIM)
    iw = iw * IDX_HEADS ** -0.5
    key_pos = jnp.arange(S, dtype=jnp.int32)

    def blocks(t):
        return jnp.moveaxis(t.reshape(B, nb, Q_BLOCK, *t.shape[2:]), 1, 0)

    def one_block(args):
        q_lat_b, iq_b, iw_b, start = args
        t_pos = start + jnp.arange(Q_BLOCK, dtype=jnp.int32)
        s_idx = jnp.einsum('bqhd,bsd->bqhs', iq_b, ik) * IDX_DIM ** -0.5
        score = jnp.einsum('bqhs,bqh->bqs', jax.nn.relu(s_idx), iw_b).astype(jnp.float32)
        score = jnp.where(key_pos[None, :] <= t_pos[:, None], score, -jnp.inf)
        _, sel = lax.top_k(score, K)
        valid = sel <= t_pos[None, :, None]
        c_sel = jax.vmap(lambda c, idx: c[idx])(ckv, sel)
        logits = jnp.einsum('bqhr,bqkr->bqhk', q_lat_b, c_sel).astype(jnp.float32)
        bias = rel_bias[t5_bucket(t_pos[None, :, None] - sel)]
        logits = logits + jnp.moveaxis(bias, -1, 2).astype(jnp.float32)
        logits = jnp.where(valid[:, :, None, :], logits, -jnp.inf)
        p = jax.nn.softmax(logits, axis=-1).astype(c_sel.dtype)
        o_lat = jnp.einsum('bqhk,bqkr->bqhr', p, c_sel)
        return jnp.einsum('bqhr,hrd->bqhd', o_lat, w_uv)

    starts = jnp.arange(nb, dtype=jnp.int32) * Q_BLOCK
    o = lax.map(one_block, (blocks(q_lat), blocks(iq), blocks(iw), starts))
    return jnp.moveaxis(o, 0, 1).reshape(B, S, AT_WIDTH)


def memory_cross_attention(x, mem, w_q, w_k, w_v, w_o):
    B, S, D = x.shape
    M = mem.shape[1]
    q = (x @ w_q).reshape(B, S, MEM_HEADS, MEM_DH)
    k = (mem @ w_k).reshape(B, M, MEM_HEADS, MEM_DH)
    v = (mem @ w_v).reshape(B, M, MEM_HEADS, MEM_DH)
    logits = jnp.einsum('bshd,bmhd->bhsm', q, k).astype(jnp.float32) * MEM_DH ** -0.5
    p = jax.nn.softmax(logits, axis=-1).astype(v.dtype)
    o = jnp.einsum('bhsm,bmhd->bshd', p, v).reshape(B, S, D)
    return o @ w_o


def peer_ffn(x, w_query, sub_keys, expert_u, expert_v):
    B, S, D = x.shape
    T = B * S
    xt = x.reshape(T, D)
    q = (xt @ w_query).reshape(T, PEER_HEADS, 2, PEER_DKEY // 2)
    s = jnp.einsum('thpd,hpnd->thpn', q, sub_keys).astype(jnp.float32)
    v_half, i_half = lax.top_k(s, PEER_TOPK)
    cand = v_half[:, :, 0, :, None] + v_half[:, :, 1, None, :]
    cand_idx = i_half[:, :, 0, :, None] * PEER_NKEYS + i_half[:, :, 1, None, :]
    top_s, top_pos = lax.top_k(cand.reshape(T, PEER_HEADS, -1), PEER_TOPK)
    experts = jnp.take_along_axis(cand_idx.reshape(T, PEER_HEADS, -1), top_pos, axis=-1)
    gate = jax.nn.softmax(top_s, axis=-1).astype(x.dtype)
    E = PEER_HEADS * PEER_TOPK
    nb = T // PEER_BLOCK

    def one_block(args):
        xb, eb, gb = args
        h = jax.nn.gelu(jnp.einsum('td,ted->te', xb, expert_u[eb]), approximate=False)
        return jnp.einsum('te,ted->td', gb * h, expert_v[eb])

    y = lax.map(one_block, (xt.reshape(nb, PEER_BLOCK, D), experts.reshape(nb, PEER_BLOCK, E),
                            gate.reshape(nb, PEER_BLOCK, E)))
    return y.reshape(B, S, D)


def setup_inputs(seed: int = 0) -> dict:
    key = jax.random.key(seed)
    ks = jax.random.split(key, 32)
    L, D = DEPTH, D_MODEL
    nrm = lambda k, shape, s: jax.random.normal(k, shape, jnp.float32) * s
    return {
        'x': nrm(ks[0], (BATCH, SEQ, D), 1.0),
        'mem': nrm(ks[1], (BATCH, MEM_LEN, D), 1.0),
        'w_in': nrm(ks[2], (L, D, IN_COLS), D ** -0.5),
        'lb_logits': nrm(ks[3], (L + 1, HG_WIDTH), 0.1),
        'hg_norm_g': 1.0 + nrm(ks[4], (L, HG_WIDTH), 0.02),
        'kv_norm_g': 1.0 + nrm(ks[5], (L, KV_RANK), 0.02),
        'ik_norm_g': 1.0 + nrm(ks[6], (L, IDX_DIM), 0.02),
        'ik_norm_b': nrm(ks[7], (L, IDX_DIM), 0.02),
        'w_uk': nrm(ks[8], (L, AT_HEADS, KV_RANK, AT_DH), KV_RANK ** -0.5),
        'w_uv': nrm(ks[9], (L, AT_HEADS, KV_RANK, AT_DH), KV_RANK ** -0.5 * BETA),
        'rel_bias': nrm(ks[10], (REL_BUCKETS, AT_HEADS), 0.5),
        'w_branch_a': nrm(ks[11], (L, HG_WIDTH, D), HG_WIDTH ** -0.5 * BETA),
        'w_branch_b': nrm(ks[12], (L, AT_WIDTH, D), AT_WIDTH ** -0.5 * BETA),
        'w_mix_out': nrm(ks[13], (L, D, D), D ** -0.5 * BETA),
        'ln1_g': 1.0 + nrm(ks[14], (L, D), 0.02),
        'ln1_b': nrm(ks[15], (L, D), 0.02),
        'w_cq': nrm(ks[16], (L, D, D), D ** -0.5),
        'w_ck': nrm(ks[17], (L, D, D), D ** -0.5),
        'w_cv': nrm(ks[18], (L, D, D), D ** -0.5 * BETA),
        'w_co': nrm(ks[19], (L, D, D), D ** -0.5 * BETA),
        'ln2_g': 1.0 + nrm(ks[20], (L, D), 0.02),
        'ln2_b': nrm(ks[21], (L, D), 0.02),
        'w_pq': nrm(ks[22], (L, D, PEER_HEADS * PEER_DKEY), D ** -0.5),
        'peer_keys': nrm(ks[23], (L, PEER_HEADS, 2, PEER_NKEYS, PEER_DKEY // 2), (PEER_DKEY // 2) ** -0.5),
        'peer_u': nrm(ks[24], (L, PEER_N, D), D ** -0.5),
        'peer_v': nrm(ks[25], (L, PEER_N, D), BETA),
        'ln3_g': 1.0 + nrm(ks[26], (L, D), 0.02),
        'ln3_b': nrm(ks[27], (L, D), 0.02),
    }


def reference(x, mem, w_in, lb_logits, hg_norm_g, kv_norm_g, ik_norm_g, ik_norm_b, w_uk, w_uv,
              rel_bias, w_branch_a, w_branch_b, w_mix_out, ln1_g, ln1_b, w_cq, w_ck, w_cv, w_co,
              ln2_g, ln2_b, w_pq, peer_keys, peer_u, peer_v, ln3_g, ln3_b):
    lower_bounds = jnp.cumsum(jax.nn.softmax(lb_logits.astype(jnp.float32), axis=0), axis=0)
    h = x
    for l in range(DEPTH):
        proj = h @ w_in[l]
        hq, hf, hi, hg, aq, ackv, aiq, aik, aiw, gate_a, gate_b = split_columns(proj, IN_SIZES)
        ya = hgrn2_branch(hq, hf, hi, hg, lower_bounds[l], hg_norm_g[l]) @ w_branch_a[l]
        yb = dsa_branch(aq, ackv, aiq, aik, aiw, kv_norm_g[l], ik_norm_g[l], ik_norm_b[l],
                        w_uk[l], w_uv[l], rel_bias) @ w_branch_b[l]
        mix = (jax.nn.sigmoid(gate_a) * ya + jax.nn.sigmoid(gate_b) * yb) @ w_mix_out[l]
        h = layer_norm(ALPHA * h + mix, ln1_g[l], ln1_b[l])
        ca = memory_cross_attention(h, mem, w_cq[l], w_ck[l], w_cv[l], w_co[l])
        h = layer_norm(ALPHA * h + ca, ln2_g[l], ln2_b[l])
        ff = peer_ffn(h, w_pq[l], peer_keys[l], peer_u[l], peer_v[l])
        h = layer_norm(ALPHA * h + ff, ln3_g[l], ln3_b[l])
    return h
```

```python
import functools
import math

import jax
import jax.numpy as jnp
from jax import lax
from jax.experimental import pallas as pl
from jax.experimental.pallas import tpu as pltpu

F32 = jnp.float32
BF16 = jnp.bfloat16

D_MODEL = 2048
ALPHA = 2.0 ** 0.25
LN_EPS = 1e-5
RMS_EPS = 1e-6
HG_DK = 128
HG_HEADS = D_MODEL // 128
HG_WIDTH = HG_HEADS * HG_DK
HG_CHUNK = 64
AT_DH = 128
AT_HEADS = D_MODEL // 128
AT_WIDTH = AT_HEADS * AT_DH
KV_RANK = D_MODEL // 8
IDX_HEADS = 16
IDX_DIM = 128
TOPK_MAX = 256
REL_BUCKETS = 32
REL_MAX_DIST = 128
MEM_HEADS = 4
MEM_DH = D_MODEL // MEM_HEADS
PEER_HEADS = 8
PEER_NKEYS = 128
PEER_N = PEER_NKEYS * PEER_NKEYS
PEER_DKEY = 256
PEER_TOPK = 16

LANES = 128
VMEM_LIMIT = 56 * 1024 * 1024

C_HQ, C_HF, C_HI, C_HG = 0, HG_WIDTH, 2 * HG_WIDTH, 3 * HG_WIDTH
C_CKV = 4 * HG_WIDTH
C_IK = C_CKV + KV_RANK
C_IW = C_IK + IDX_DIM
C_GA = C_IW + LANES
C_GB = C_GA + D_MODEL
N_P32 = C_GB + D_MODEL

NEG_INF = float("-inf")


def _cparams(sem):
    return pltpu.CompilerParams(dimension_semantics=sem, vmem_limit_bytes=VMEM_LIMIT)


def _dot(a, b):
    return jnp.dot(a, b, preferred_element_type=F32)


def _dot_nt(a, b):
    return lax.dot_general(a, b, (((1,), (1,)), ((), ())), preferred_element_type=F32)


def _dot_tn(a, b):
    return lax.dot_general(a, b, (((0,), (0,)), ((), ())), preferred_element_type=F32)


def _mm_kernel(a_ref, w_ref, o_ref):
    o_ref[...] = _dot(a_ref[...].astype(BF16), w_ref[...]).astype(o_ref.dtype)


def _matmul(a, w, tm, tn, out_dtype):
    m, k = a.shape
    n = w.shape[1]
    return pl.pallas_call(
        _mm_kernel,
        grid=(m // tm, n // tn),
        in_specs=[pl.BlockSpec((tm, k), lambda i, j: (i, 0)),
                  pl.BlockSpec((k, tn), lambda i, j: (0, j))],
        out_specs=pl.BlockSpec((tm, tn), lambda i, j: (i, j)),
        out_shape=jax.ShapeDtypeStruct((m, n), out_dtype),
        compiler_params=_cparams(("parallel", "parallel")),
    )(a, w)


def _layer_norm_rows(y, g, b):
    mu = jnp.mean(y, axis=-1, keepdims=True)
    yc = y - mu
    var = jnp.mean(yc * yc, axis=-1, keepdims=True)
    return yc * lax.rsqrt(var + LN_EPS) * g + b


def _mm_res_ln_kernel(a_ref, w_ref, res_ref, g_ref, b_ref, o_ref):
    y = ALPHA * res_ref[...] + _dot(a_ref[...].astype(BF16), w_ref[...])
    o_ref[...] = _layer_norm_rows(y, g_ref[...], b_ref[...])


def _matmul_res_ln(a, w, res, g, b, tm):
    m, k = a.shape
    n = w.shape[1]
    return pl.pallas_call(
        _mm_res_ln_kernel,
        grid=(m // tm,),
        in_specs=[pl.BlockSpec((tm, k), lambda i: (i, 0)),
                  pl.BlockSpec((k, n), lambda i: (0, 0)),
                  pl.BlockSpec((tm, n), lambda i: (i, 0)),
                  pl.BlockSpec((1, n), lambda i: (0, 0)),
                  pl.BlockSpec((1, n), lambda i: (0, 0))],
        out_specs=pl.BlockSpec((tm, n), lambda i: (i, 0)),
        out_shape=jax.ShapeDtypeStruct((m, n), F32),
        compiler_params=_cparams(("parallel",)),
    )(a, w, res, g.reshape(1, n), b.reshape(1, n))


def _merge_kernel(a1_ref, a2_ref, w1_ref, w2_ref, ga_ref, gb_ref, o_ref):
    ya = _dot(a1_ref[...], w1_ref[...])
    yb = _dot(a2_ref[...], w2_ref[...])
    o_ref[...] = (jax.nn.sigmoid(ga_ref[...]) * ya + jax.nn.sigmoid(gb_ref[...]) * yb).astype(o_ref.dtype)


def _merge(a1, a2, w1, w2, p32, tm, tn):
    m, k = a1.shape
    n = w1.shape[1]
    ga_blk, gb_blk = C_GA // tn, C_GB // tn
    return pl.pallas_call(
        _merge_kernel,
        grid=(m // tm, n // tn),
        in_specs=[pl.BlockSpec((tm, k), lambda i, j: (i, 0)),
                  pl.BlockSpec((tm, k), lambda i, j: (i, 0)),
                  pl.BlockSpec((k, tn), lambda i, j: (0, j)),
                  pl.BlockSpec((k, tn), lambda i, j: (0, j)),
                  pl.BlockSpec((tm, tn), lambda i, j: (i, ga_blk + j)),
                  pl.BlockSpec((tm, tn), lambda i, j: (i, gb_blk + j))],
        out_specs=pl.BlockSpec((tm, tn), lambda i, j: (i, j)),
        out_shape=jax.ShapeDtypeStruct((m, n), BF16),
        compiler_params=_cparams(("parallel", "parallel")),
    )(a1, a2, w1, w2, p32, p32)


def _hgrn_kernel(q_ref, f_ref, i_ref, g_ref, lb_ref, ng_ref, o_ref, st_ref, *, hb, rb):
    c = HG_CHUNK

    @pl.when(pl.program_id(2) == 0)
    def _():
        st_ref[...] = jnp.zeros_like(st_ref)

    l0 = lb_ref[0:1, :]
    l1 = lb_ref[1:2, :]
    mx = jnp.maximum(l0, l1)
    e0 = jnp.exp(l0 - mx)
    lb = e0 / (e0 + jnp.exp(l1 - mx))

    f = lb + (1.0 - lb) * jax.nn.sigmoid(f_ref[...])
    kk = 1.0 - f
    gcum = jnp.log(f)
    pos = lax.broadcasted_iota(jnp.int32, gcum.shape, 0) & (c - 1)
    for s in (1, 2, 4, 8, 16, 32):
        gcum = gcum + jnp.where(pos >= s, pltpu.roll(gcum, s, axis=0), 0.0)

    tril = (lax.broadcasted_iota(jnp.int32, (c, c), 0) >= lax.broadcasted_iota(jnp.int32, (c, c), 1))
    ng = ng_ref[...]
    for ci in range(rb // c):
        rows = slice(ci * c, (ci + 1) * c)
        for h in range(hb):
            cols = slice(h * HG_DK, (h + 1) * HG_DK)
            gc = gcum[rows, cols]
            g_ref_row = gc[c // 2 - 1:c // 2, :]
            g_last = gc[c - 1:c, :]
            qc = q_ref[rows, cols]
            kc = kk[rows, cols]
            vc = i_ref[rows, cols].astype(BF16)
            qa = (qc * jnp.exp(gc - g_ref_row)).astype(BF16)
            kb = (kc * jnp.exp(g_ref_row - gc)).astype(BF16)
            a = jnp.where(tril, _dot_nt(qa, kb), 0.0)
            o = _dot(a.astype(BF16), vc)
            st = st_ref[h]
            qd = (qc * jnp.exp(gc)).astype(BF16)
            o = o + _dot_nt(qd, st.astype(BF16))
            kd = (kc * jnp.exp(g_last - gc)).astype(BF16)
            st_ref[h] = st * jnp.exp(g_last) + _dot_tn(vc, kd)
            o = o * lax.rsqrt(jnp.mean(o * o, axis=-1, keepdims=True) + RMS_EPS) * ng[:, cols]
            gate = g_ref[rows, cols]
            o_ref[rows, cols] = (o * (gate * jax.nn.sigmoid(gate))).astype(o_ref.dtype)


def _hgrn(p32, lb_logits, norm_g, batch, seq, hb=4, rb=512):
    t = batch * seq
    w = hb * HG_DK
    nr = seq // rb
    hq, hf, hi, hg = (cc // w for cc in (C_HQ, C_HF, C_HI, C_HG))
    row = lambda b, h, r: b * nr + r
    in_spec = lambda off: pl.BlockSpec((rb, w), lambda b, h, r: (row(b, h, r), off + h))
    return pl.pallas_call(
        functools.partial(_hgrn_kernel, hb=hb, rb=rb),
        grid=(batch, HG_HEADS // hb, nr),
        in_specs=[in_spec(hq), in_spec(hf), in_spec(hi), in_spec(hg),
                  pl.BlockSpec((2, w), lambda b, h, r: (0, h)),
                  pl.BlockSpec((1, w), lambda b, h, r: (0, h))],
        out_specs=pl.BlockSpec((rb, w), lambda b, h, r: (row(b, h, r), h)),
        out_shape=jax.ShapeDtypeStruct((t, HG_WIDTH), BF16),
        scratch_shapes=[pltpu.VMEM((hb, HG_DK, HG_DK), F32)],
        compiler_params=_cparams(("parallel", "parallel", "arbitrary")),
    )(p32, p32, p32, p32, lb_logits, norm_g.reshape(1, HG_WIDTH))


def _dsa_prep_kernel(ckv_ref, ik_ref, aq_ref, kvg_ref, ikg_ref, ikb_ref, wuk_ref,
                     ckvn_ref, ikn_ref, qlat_ref):
    ckv = ckv_ref[...]
    ckvn_ref[...] = (ckv * lax.rsqrt(jnp.mean(ckv * ckv, axis=-1, keepdims=True) + RMS_EPS)
                     * kvg_ref[...]).astype(ckvn_ref.dtype)
    ikn_ref[...] = _layer_norm_rows(ik_ref[...], ikg_ref[...], ikb_ref[...]).astype(ikn_ref.dtype)
    scale = AT_DH ** -0.5
    for h in range(AT_HEADS):
        qh = aq_ref[:, h * AT_DH:(h + 1) * AT_DH]
        qlat_ref[:, h * KV_RANK:(h + 1) * KV_RANK] = (_dot_nt(qh, wuk_ref[h]) * scale).astype(qlat_ref.dtype)


def _dsa_prep(p32, p16, kv_g, ik_g, ik_b, w_uk, tm):
    t = p32.shape[0]
    return pl.pallas_call(
        _dsa_prep_kernel,
        grid=(t // tm,),
        in_specs=[pl.BlockSpec((tm, KV_RANK), lambda i: (i, C_CKV // KV_RANK)),
                  pl.BlockSpec((tm, IDX_DIM), lambda i: (i, C_IK // IDX_DIM)),
                  pl.BlockSpec((tm, AT_WIDTH), lambda i: (i, 0)),
                  pl.BlockSpec((1, KV_RANK), lambda i: (0, 0)),
                  pl.BlockSpec((1, IDX_DIM), lambda i: (0, 0)),
                  pl.BlockSpec((1, IDX_DIM), lambda i: (0, 0)),
                  pl.BlockSpec((AT_HEADS, KV_RANK, AT_DH), lambda i: (0, 0, 0))],
        out_specs=[pl.BlockSpec((tm, KV_RANK), lambda i: (i, 0)),
                   pl.BlockSpec((tm, IDX_DIM), lambda i: (i, 0)),
                   pl.BlockSpec((tm, AT_HEADS * KV_RANK), lambda i: (i, 0))],
        out_shape=[jax.ShapeDtypeStruct((t, KV_RANK), BF16),
                   jax.ShapeDtypeStruct((t, IDX_DIM), BF16),
                   jax.ShapeDtypeStruct((t, AT_HEADS * KV_RANK), BF16)],
        compiler_params=_cparams(("parallel",)),
    )(p32, p32, p16, kv_g.reshape(1, -1), ik_g.reshape(1, -1), ik_b.reshape(1, -1), w_uk)


def _dsa_kernel(iq_ref, iwt_ref, ik_ref, qlat_ref, ckv_ref, ckvt_ref, wuvt_ref, near_ref,
                o_ref, key_ref, lg_ref, *, tq, seq, topk):
    i = pl.program_id(1)
    t0 = i * tq
    ik = ik_ref[...]
    w_scale = (IDX_DIM ** -0.5) * (IDX_HEADS ** -0.5)
    sc = jnp.zeros((seq, tq), F32)
    for h in range(IDX_HEADS):
        s = _dot_nt(ik, iq_ref[:, h * IDX_DIM:(h + 1) * IDX_DIM])
        sc = sc + jnp.maximum(s, 0.0) * (iwt_ref[h:h + 1, :] * w_scale)
    kpos = lax.broadcasted_iota(jnp.int32, (seq, tq), 0)
    tpos = t0 + lax.broadcasted_iota(jnp.int32, (seq, tq), 1)
    causal = kpos <= tpos
    sc = jnp.where(causal, sc + 0.0, NEG_INF)
    bits = pltpu.bitcast(sc, jnp.int32)
    key_ref[...] = bits ^ ((bits >> 31) & jnp.int32(0x7FFFFFFF))

    def bit_step(it, thr):
        cand = thr + jnp.left_shift(jnp.int32(1), 31 - it)
        cnt = jnp.sum((key_ref[...] >= cand).astype(F32), axis=0, keepdims=True)
        return jnp.where(cnt >= float(topk), cand, thr)

    thr = lax.fori_loop(0, 32, bit_step, jnp.full((1, tq), jnp.iinfo(jnp.int32).min, jnp.int32))
    addmask = jnp.where((key_ref[...] >= thr) & causal, 0.0, NEG_INF)

    wn = tq + REL_MAX_DIST
    for h in range(AT_HEADS):
        lg_ref[...] = _dot_nt(ckv_ref[...], qlat_ref[:, h * KV_RANK:(h + 1) * KV_RANK]) + addmask

        @pl.when(i == 0)
        def _():
            lg_ref[0:tq, :] += near_ref[h, REL_MAX_DIST:wn, :]

        @pl.when(i > 0)
        def _():
            start = pl.multiple_of(t0 - REL_MAX_DIST, REL_MAX_DIST)
            lg_ref[pl.ds(start, wn), :] += near_ref[h]

        lg = lg_ref[...]
        p = jnp.exp(lg - jnp.max(lg, axis=0, keepdims=True))
        denom = jnp.sum(p, axis=0, keepdims=True)
        olat = _dot(ckvt_ref[...], p.astype(BF16)) / denom
        out_t = _dot(wuvt_ref[h], olat.astype(BF16))
        o_ref[:, h * AT_DH:(h + 1) * AT_DH] = out_t.T.astype(o_ref.dtype)


def _near_bias_table(rel_bias, tq):
    wn = tq + REL_MAX_DIST
    d = (jnp.arange(tq, dtype=jnp.int32)[None, :] - jnp.arange(wn, dtype=jnp.int32)[:, None]
         + REL_MAX_DIST)
    max_exact = REL_BUCKETS // 2
    dd = jnp.maximum(d, 0)
    large = max_exact + (jnp.log(jnp.maximum(dd, 1).astype(F32) / max_exact)
                         / math.log(REL_MAX_DIST / max_exact) * (REL_BUCKETS - max_exact)).astype(jnp.int32)
    large = jnp.minimum(large, REL_BUCKETS - 1)
    bucket = jnp.where(dd < max_exact, dd, large)
    tab = rel_bias.astype(F32)[bucket] - rel_bias.astype(F32)[REL_BUCKETS - 1]
    return jnp.moveaxis(tab, -1, 0)


def _dsa(p16, iw_t, ikn, qlat, ckvn, ckvn_t, w_uv_t, near, batch, seq, tq):
    t = batch * seq
    nq = seq // tq
    topk = min(TOPK_MAX, seq // 4)
    wn = tq + REL_MAX_DIST
    return pl.pallas_call(
        functools.partial(_dsa_kernel, tq=tq, seq=seq, topk=topk),
        grid=(batch, nq),
        in_specs=[pl.BlockSpec((tq, IDX_HEADS * IDX_DIM), lambda b, i: (b * nq + i, 1)),
                  pl.BlockSpec((IDX_HEADS, tq), lambda b, i: (0, b * nq + i)),
                  pl.BlockSpec((seq, IDX_DIM), lambda b, i: (b, 0)),
                  pl.BlockSpec((tq, AT_HEADS * KV_RANK), lambda b, i: (b * nq + i, 0)),
                  pl.BlockSpec((seq, KV_RANK), lambda b, i: (b, 0)),
                  pl.BlockSpec((None, KV_RANK, seq), lambda b, i: (b, 0, 0)),
                  pl.BlockSpec((AT_HEADS, AT_DH, KV_RANK), lambda b, i: (0, 0, 0)),
                  pl.BlockSpec((AT_HEADS, wn, tq), lambda b, i: (0, 0, 0))],
        out_specs=pl.BlockSpec((tq, AT_WIDTH), lambda b, i: (b * nq + i, 0)),
        out_shape=jax.ShapeDtypeStruct((t, AT_WIDTH), BF16),
        scratch_shapes=[pltpu.VMEM((seq, tq), jnp.int32), pltpu.VMEM((seq, tq), F32)],
        compiler_params=_cparams(("parallel", "arbitrary")),
    )(p16, iw_t, ikn, qlat, ckvn, ckvn_t, w_uv_t, near)


def _xattn_kernel(q_ref, k_ref, v_ref, o_ref):
    scale = MEM_DH ** -0.5
    for h in range(MEM_HEADS):
        cols = slice(h * MEM_DH, (h + 1) * MEM_DH)
        lg = _dot_nt(q_ref[:, cols], k_ref[:, cols]) * scale
        p = jnp.exp(lg - jnp.max(lg, axis=-1, keepdims=True))
        p = p / jnp.sum(p, axis=-1, keepdims=True)
        o_ref[:, cols] = _dot(p.astype(BF16), v_ref[:, cols]).astype(o_ref.dtype)


def _xattn(q, k, v, batch, seq, mlen, tq):
    t = batch * seq
    nq = seq // tq
    return pl.pallas_call(
        _xattn_kernel,
        grid=(batch, nq),
        in_specs=[pl.BlockSpec((tq, D_MODEL), lambda b, i: (b * nq + i, 0)),
                  pl.BlockSpec((mlen, D_MODEL), lambda b, i: (b, 0)),
                  pl.BlockSpec((mlen, D_MODEL), lambda b, i: (b, 0))],
        out_specs=pl.BlockSpec((tq, D_MODEL), lambda b, i: (b * nq + i, 0)),
        out_shape=jax.ShapeDtypeStruct((t, D_MODEL), BF16),
        compiler_params=_cparams(("parallel", "parallel")),
    )(q, k, v)


def _top_values(x, k):
    rows = lax.broadcasted_iota(jnp.int32, (k, x.shape[1]), 0)
    out = jnp.zeros((k, x.shape[1]), F32)
    for a in range(k):
        mx = jnp.max(x, axis=0, keepdims=True)
        out = jnp.where(rows == a, mx, out)
        x = jnp.where(x == mx, NEG_INF, x)
    return out


def _peer_kernel(qp_ref, keys_ref, x_ref, u_ref, vt_ref, res_ref, g_ref, b_ref, o_ref,
                 s_ref, e_ref, th_ref, acc_ref, *, tt, te):
    e = pl.program_id(1)
    nk = PEER_NKEYS
    k = PEER_TOPK

    @pl.when(e == 0)
    def _():
        acc_ref[...] = jnp.zeros_like(acc_ref)
        for hp in range(2 * PEER_HEADS):
            s_ref[hp] = _dot_nt(keys_ref[hp], qp_ref[:, hp * nk:(hp + 1) * nk])
        for h in range(PEER_HEADS):
            s0 = s_ref[2 * h]
            s1 = s_ref[2 * h + 1]
            v0 = _top_values(s0, k)
            v1 = _top_values(s1, k)
            cand = jnp.concatenate([v0[a:a + 1, :] + v1 for a in range(k)], axis=0)
            top = _top_values(cand, k)
            z = jnp.sum(jnp.exp(top - top[0:1, :]), axis=0, keepdims=True)
            th_ref[h:h + 1, :] = top[k - 1:k, :]
            e_ref[2 * h] = jnp.exp(s0 - v0[0:1, :]) / z
            e_ref[2 * h + 1] = jnp.exp(s1 - v1[0:1, :])

    ht = _dot_nt(u_ref[...], x_ref[...])
    gel = 0.5 * ht * (1.0 + lax.erf(ht * (2.0 ** -0.5)))
    parts = []
    for ii in range(te // nk):
        irow = e * (te // nk) + ii
        w = jnp.zeros((nk, tt), F32)
        for h in range(PEER_HEADS):
            s0row = s_ref[2 * h, pl.ds(irow, 1), :]
            e0row = e_ref[2 * h, pl.ds(irow, 1), :]
            ssum = s0row + s_ref[2 * h + 1]
            w = w + jnp.where(ssum >= th_ref[h:h + 1, :], e0row * e_ref[2 * h + 1], 0.0)
        parts.append(w)
    wfull = parts[0] if len(parts) == 1 else jnp.concatenate(parts, axis=0)
    acc_ref[...] += _dot(vt_ref[...], (wfull * gel).astype(BF16))

    @pl.when(e == pl.num_programs(1) - 1)
    def _():
        tc = 256
        for c in range(tt // tc):
            rows = slice(c * tc, (c + 1) * tc)
            y = ALPHA * res_ref[rows, :] + acc_ref[:, rows].T
            o_ref[rows, :] = _layer_norm_rows(y, g_ref[...], b_ref[...])


def _peer(qp, keys, h2, u, v_t, g, b, tt, te):
    t, d = h2.shape
    n = u.shape[0]
    return pl.pallas_call(
        functools.partial(_peer_kernel, tt=tt, te=te),
        grid=(t // tt, n // te),
        in_specs=[pl.BlockSpec((tt, PEER_HEADS * PEER_DKEY), lambda i, e: (i, 0)),
                  pl.BlockSpec((2 * PEER_HEADS, PEER_NKEYS, PEER_DKEY // 2), lambda i, e: (0, 0, 0)),
                  pl.BlockSpec((tt, d), lambda i, e: (i, 0)),
                  pl.BlockSpec((te, d), lambda i, e: (e, 0)),
                  pl.BlockSpec((d, te), lambda i, e: (0, e)),
                  pl.BlockSpec((tt, d), lambda i, e: (i, 0)),
                  pl.BlockSpec((1, d), lambda i, e: (0, 0)),
                  pl.BlockSpec((1, d), lambda i, e: (0, 0))],
        out_specs=pl.BlockSpec((tt, d), lambda i, e: (i, 0)),
        out_shape=jax.ShapeDtypeStruct((t, d), F32),
        scratch_shapes=[pltpu.VMEM((2 * PEER_HEADS, PEER_NKEYS, tt), F32),
                        pltpu.VMEM((2 * PEER_HEADS, PEER_NKEYS, tt), F32),
                        pltpu.VMEM((PEER_HEADS, tt), F32),
                        pltpu.VMEM((d, tt), F32)],
        compiler_params=_cparams(("parallel", "arbitrary")),
    )(qp, keys, h2.astype(BF16), u, v_t, h2, g.reshape(1, d), b.reshape(1, d))


def kernel(x, mem, w_in, lb_logits, hg_norm_g, kv_norm_g, ik_norm_g, ik_norm_b, w_uk, w_uv,
           rel_bias, w_branch_a, w_branch_b, w_mix_out, ln1_g, ln1_b, w_cq, w_ck, w_cv, w_co,
           ln2_g, ln2_b, w_pq, peer_keys, peer_u, peer_v, ln3_g, ln3_b):
    batch, seq, d = x.shape
    mlen = mem.shape[1]
    t = batch * seq
    l = 0

    wi = w_in[l]
    o_aq = 4 * HG_WIDTH
    o_ckv = o_aq + AT_WIDTH
    o_iq = o_ckv + KV_RANK
    o_ik = o_iq + IDX_HEADS * IDX_DIM
    o_iw = o_ik + IDX_DIM
    o_ga = o_iw + IDX_HEADS
    w32 = jnp.concatenate([wi[:, :o_aq], wi[:, o_ckv:o_iq], wi[:, o_ik:o_iw],
                           jnp.pad(wi[:, o_iw:o_ga], ((0, 0), (0, LANES - IDX_HEADS))),
                           wi[:, o_ga:]], axis=1).astype(BF16)
    w16 = jnp.concatenate([wi[:, o_aq:o_ckv], wi[:, o_iq:o_ik]], axis=1).astype(BF16)
    xf = x.reshape(t, d)
    xb = xf.astype(BF16)

    p32 = _matmul(xb, w32, 2048, 512, F32)
    p16 = _matmul(xb, w16, 2048, 1024, BF16)
    ya_in = _hgrn(p32, lb_logits, hg_norm_g[l], batch, seq)
    ckvn, ikn, qlat = _dsa_prep(p32, p16, kv_norm_g[l], ik_norm_g[l], ik_norm_b[l],
                                w_uk[l].astype(BF16), 1024)
    tq = 256
    iw_t = p32[:, C_IW:C_IW + IDX_HEADS].T
    ckvn_t = jnp.swapaxes(ckvn.reshape(batch, seq, KV_RANK), 1, 2)
    w_uv_t = jnp.swapaxes(w_uv[l], 1, 2).astype(BF16)
    near = _near_bias_table(rel_bias, tq)
    yb_in = _dsa(p16, iw_t, ikn, qlat, ckvn, ckvn_t, w_uv_t, near, batch, seq, tq)
    mix_in = _merge(ya_in, yb_in, w_branch_a[l].astype(BF16), w_branch_b[l].astype(BF16), p32, 1024, 512)
    h1 = _matmul_res_ln(mix_in, w_mix_out[l].astype(BF16), xf, ln1_g[l], ln1_b[l], 512)

    qc = _matmul(h1, w_cq[l].astype(BF16), 1024, 1024, BF16)
    memf = mem.reshape(batch * mlen, d).astype(BF16)
    kc = _matmul(memf, w_ck[l].astype(BF16), 1024, 1024, BF16)
    vc = _matmul(memf, w_cv[l].astype(BF16), 1024, 1024, BF16)
    ca_in = _xattn(qc, kc, vc, batch, seq, mlen, 512)
    h2 = _matmul_res_ln(ca_in, w_co[l].astype(BF16), h1, ln2_g[l], ln2_b[l], 512)

    qp = _matmul(h2, w_pq[l].astype(BF16), 1024, 1024, BF16)
    keys = peer_keys[l].reshape(2 * PEER_HEADS, PEER_NKEYS, PEER_DKEY // 2).astype(BF16)
    u = peer_u[l].astype(BF16)
    v_t = peer_v[l].T.astype(BF16)
    h3 = _peer(qp, keys, h2, u, v_t, ln3_g[l], ln3_b[l], 512, 256)
    return h3.reshape(batch, seq, d)
```

```python
import functools
import math

import jax
import jax.numpy as jnp
from jax import lax
from jax.experimental import pallas as pl
from jax.experimental.pallas import tpu as pltpu

F32 = jnp.float32
BF16 = jnp.bfloat16

D_MODEL = 2048
ALPHA = 2.0 ** 0.25
LN_EPS = 1e-5
RMS_EPS = 1e-6
HG_DK = 128
HG_HEADS = D_MODEL // 128
HG_WIDTH = HG_HEADS * HG_DK
HG_CHUNK = 64
AT_DH = 128
AT_HEADS = D_MODEL // 128
AT_WIDTH = AT_HEADS * AT_DH
KV_RANK = D_MODEL // 8
IDX_HEADS = 16
IDX_DIM = 128
TOPK_MAX = 256
REL_BUCKETS = 32
REL_MAX_DIST = 128
MEM_HEADS = 4
MEM_DH = D_MODEL // MEM_HEADS
PEER_HEADS = 8
PEER_NKEYS = 128
PEER_N = PEER_NKEYS * PEER_NKEYS
PEER_DKEY = 256
PEER_TOPK = 16

LANES = 128
VMEM_LIMIT = 56 * 1024 * 1024

C_HQ, C_HF, C_HI, C_HG = 0, HG_WIDTH, 2 * HG_WIDTH, 3 * HG_WIDTH
C_CKV = 4 * HG_WIDTH
C_IK = C_CKV + KV_RANK
C_IW = C_IK + IDX_DIM
C_GA = C_IW + LANES
C_GB = C_GA + D_MODEL
N_P32 = C_GB + D_MODEL

NEG_INF = float("-inf")


def _cparams(sem):
    return pltpu.CompilerParams(dimension_semantics=sem, vmem_limit_bytes=VMEM_LIMIT)


def _dot(a, b):
    return jnp.dot(a, b, preferred_element_type=F32)


def _dot_nt(a, b):
    return lax.dot_general(a, b, (((1,), (1,)), ((), ())), preferred_element_type=F32)


def _dot_tn(a, b):
    return lax.dot_general(a, b, (((0,), (0,)), ((), ())), preferred_element_type=F32)


def _mm_kernel(a_ref, w_ref, o_ref):
    o_ref[...] = _dot(a_ref[...].astype(BF16), w_ref[...]).astype(o_ref.dtype)


def _matmul(a, w, tm, tn, out_dtype):
    m, k = a.shape
    n = w.shape[1]
    return pl.pallas_call(
        _mm_kernel,
        grid=(m // tm, n // tn),
        in_specs=[pl.BlockSpec((tm, k), lambda i, j: (i, 0)),
                  pl.BlockSpec((k, tn), lambda i, j: (0, j))],
        out_specs=pl.BlockSpec((tm, tn), lambda i, j: (i, j)),
        out_shape=jax.ShapeDtypeStruct((m, n), out_dtype),
        compiler_params=_cparams(("parallel", "parallel")),
    )(a, w)


def _layer_norm_rows(y, g, b):
    mu = jnp.mean(y, axis=-1, keepdims=True)
    yc = y - mu
    var = jnp.mean(yc * yc, axis=-1, keepdims=True)
    return yc * lax.rsqrt(var + LN_EPS) * g + b


def _mm_res_ln_kernel(a_ref, w_ref, res_ref, g_ref, b_ref, o_ref):
    y = ALPHA * res_ref[...] + _dot(a_ref[...].astype(BF16), w_ref[...])
    o_ref[...] = _layer_norm_rows(y, g_ref[...], b_ref[...])


def _matmul_res_ln(a, w, res, g, b, tm):
    m, k = a.shape
    n = w.shape[1]
    return pl.pallas_call(
        _mm_res_ln_kernel,
        grid=(m // tm,),
        in_specs=[pl.BlockSpec((tm, k), lambda i: (i, 0)),
                  pl.BlockSpec((k, n), lambda i: (0, 0)),
                  pl.BlockSpec((tm, n), lambda i: (i, 0)),
                  pl.BlockSpec((1, n), lambda i: (0, 0)),
                  pl.BlockSpec((1, n), lambda i: (0, 0))],
        out_specs=pl.BlockSpec((tm, n), lambda i: (i, 0)),
        out_shape=jax.ShapeDtypeStruct((m, n), F32),
        compiler_params=_cparams(("parallel",)),
    )(a, w, res, g.reshape(1, n), b.reshape(1, n))


def _merge_kernel(a1_ref, a2_ref, w1_ref, w2_ref, ga_ref, gb_ref, o_ref):
    ya = _dot(a1_ref[...], w1_ref[...])
    yb = _dot(a2_ref[...], w2_ref[...])
    o_ref[...] = (jax.nn.sigmoid(ga_ref[...]) * ya + jax.nn.sigmoid(gb_ref[...]) * yb).astype(o_ref.dtype)


def _merge(a1, a2, w1, w2, p32, tm, tn):
    m, k = a1.shape
    n = w1.shape[1]
    ga_blk, gb_blk = C_GA // tn, C_GB // tn
    return pl.pallas_call(
        _merge_kernel,
        grid=(m // tm, n // tn),
        in_specs=[pl.BlockSpec((tm, k), lambda i, j: (i, 0)),
                  pl.BlockSpec((tm, k), lambda i, j: (i, 0)),
                  pl.BlockSpec((k, tn), lambda i, j: (0, j)),
                  pl.BlockSpec((k, tn), lambda i, j: (0, j)),
                  pl.BlockSpec((tm, tn), lambda i, j: (i, ga_blk + j)),
                  pl.BlockSpec((tm, tn), lambda i, j: (i, gb_blk + j))],
        out_specs=pl.BlockSpec((tm, tn), lambda i, j: (i, j)),
        out_shape=jax.ShapeDtypeStruct((m, n), BF16),
        compiler_params=_cparams(("parallel", "parallel")),
    )(a1, a2, w1, w2, p32, p32)


def _hgrn_kernel(q_ref, f_ref, i_ref, g_ref, lb_ref, ng_ref, o_ref, st_ref, *, hb, rb):
    c = HG_CHUNK

    @pl.when(pl.program_id(2) == 0)
    def _():
        st_ref[...] = jnp.zeros_like(st_ref)

    l0 = lb_ref[0:1, :]
    l1 = lb_ref[1:2, :]
    mx = jnp.maximum(l0, l1)
    e0 = jnp.exp(l0 - mx)
    lb = e0 / (e0 + jnp.exp(l1 - mx))

    f = lb + (1.0 - lb) * jax.nn.sigmoid(f_ref[...])
    kk = 1.0 - f
    gcum = jnp.log(f)
    pos = lax.broadcasted_iota(jnp.int32, gcum.shape, 0) & (c - 1)
    for s in (1, 2, 4, 8, 16, 32):
        gcum = gcum + jnp.where(pos >= s, pltpu.roll(gcum, s, axis=0), 0.0)

    tril = (lax.broadcasted_iota(jnp.int32, (c, c), 0) >= lax.broadcasted_iota(jnp.int32, (c, c), 1))
    ng = ng_ref[...]
    for ci in range(rb // c):
        rows = slice(ci * c, (ci + 1) * c)
        for h in range(hb):
            cols = slice(h * HG_DK, (h + 1) * HG_DK)
            gc = gcum[rows, cols]
            g_ref_row = gc[c // 2 - 1:c // 2, :]
            g_last = gc[c - 1:c, :]
            qc = q_ref[rows, cols]
            kc = kk[rows, cols]
            vc = i_ref[rows, cols].astype(BF16)
            qa = (qc * jnp.exp(gc - g_ref_row)).astype(BF16)
            kb = (kc * jnp.exp(g_ref_row - gc)).astype(BF16)
            a = jnp.where(tril, _dot_nt(qa, kb), 0.0)
            o = _dot(a.astype(BF16), vc)
            st = st_ref[h]
            qd = (qc * jnp.exp(gc)).astype(BF16)
            o = o + _dot_nt(qd, st.astype(BF16))
            kd = (kc * jnp.exp(g_last - gc)).astype(BF16)
            st_ref[h] = st * jnp.exp(g_last) + _dot_tn(vc, kd)
            o = o * lax.rsqrt(jnp.mean(o * o, axis=-1, keepdims=True) + RMS_EPS) * ng[:, cols]
            gate = g_ref[rows, cols]
            o_ref[rows, cols] = (o * (gate * jax.nn.sigmoid(gate))).astype(o_ref.dtype)


def _hgrn(p32, lb_logits, norm_g, batch, seq, hb=4, rb=512):
    t = batch * seq
    w = hb * HG_DK
    nr = seq // rb
    hq, hf, hi, hg = (cc // w for cc in (C_HQ, C_HF, C_HI, C_HG))
    row = lambda b, h, r: b * nr + r
    in_spec = lambda off: pl.BlockSpec((rb, w), lambda b, h, r: (row(b, h, r), off + h))
    return pl.pallas_call(
        functools.partial(_hgrn_kernel, hb=hb, rb=rb),
        grid=(batch, HG_HEADS // hb, nr),
        in_specs=[in_spec(hq), in_spec(hf), in_spec(hi), in_spec(hg),
                  pl.BlockSpec((2, w), lambda b, h, r: (0, h)),
                  pl.BlockSpec((1, w), lambda b, h, r: (0, h))],
        out_specs=pl.BlockSpec((rb, w), lambda b, h, r: (row(b, h, r), h)),
        out_shape=jax.ShapeDtypeStruct((t, HG_WIDTH), BF16),
        scratch_shapes=[pltpu.VMEM((hb, HG_DK, HG_DK), F32)],
        compiler_params=_cparams(("parallel", "parallel", "arbitrary")),
    )(p32, p32, p32, p32, lb_logits, norm_g.reshape(1, HG_WIDTH))


def _dsa_prep_kernel(ckv_ref, ik_ref, aq_ref, kvg_ref, ikg_ref, ikb_ref, wuk_ref,
                     ckvn_ref, ikn_ref, qlat_ref):
    ckv = ckv_ref[...]
    ckvn_ref[...] = (ckv * lax.rsqrt(jnp.mean(ckv * ckv, axis=-1, keepdims=True) + RMS_EPS)
                     * kvg_ref[...]).astype(ckvn_ref.dtype)
    ikn_ref[...] = _layer_norm_rows(ik_ref[...], ikg_ref[...], ikb_ref[...]).astype(ikn_ref.dtype)
    scale = AT_DH ** -0.5
    for h in range(AT_HEADS):
        qh = aq_ref[:, h * AT_DH:(h + 1) * AT_DH]
        qlat_ref[:, h * KV_RANK:(h + 1) * KV_RANK] = (_dot_nt(qh, wuk_ref[h]) * scale).astype(qlat_ref.dtype)


def _dsa_prep(p32, p16, kv_g, ik_g, ik_b, w_uk, tm):
    t = p32.shape[0]
    return pl.pallas_call(
        _dsa_prep_kernel,
        grid=(t // tm,),
        in_specs=[pl.BlockSpec((tm, KV_RANK), lambda i: (i, C_CKV // KV_RANK)),
                  pl.BlockSpec((tm, IDX_DIM), lambda i: (i, C_IK // IDX_DIM)),
                  pl.BlockSpec((tm, AT_WIDTH), lambda i: (i, 0)),
                  pl.BlockSpec((1, KV_RANK), lambda i: (0, 0)),
                  pl.BlockSpec((1, IDX_DIM), lambda i: (0, 0)),
                  pl.BlockSpec((1, IDX_DIM), lambda i: (0, 0)),
                  pl.BlockSpec((AT_HEADS, KV_RANK, AT_DH), lambda i: (0, 0, 0))],
        out_specs=[pl.BlockSpec((tm, KV_RANK), lambda i: (i, 0)),
                   pl.BlockSpec((tm, IDX_DIM), lambda i: (i, 0)),
                   pl.BlockSpec((tm, AT_HEADS * KV_RANK), lambda i: (i, 0))],
        out_shape=[jax.ShapeDtypeStruct((t, KV_RANK), BF16),
                   jax.ShapeDtypeStruct((t, IDX_DIM), BF16),
                   jax.ShapeDtypeStruct((t, AT_HEADS * KV_RANK), BF16)],
        compiler_params=_cparams(("parallel",)),
    )(p32, p32, p16, kv_g.reshape(1, -1), ik_g.reshape(1, -1), ik_b.reshape(1, -1), w_uk)


def _dsa_kernel(iq_ref, iwt_ref, ik_ref, qlat_ref, ckv_ref, ckvt_ref, wuvt_ref, near_ref,
                o_ref, key_ref, lg_ref, *, tq, seq, topk, tile_lo):
    i = tile_lo + pl.program_id(1)
    t0 = i * tq
    ik = ik_ref[...]
    w_scale = (IDX_DIM ** -0.5) * (IDX_HEADS ** -0.5)
    sc = jnp.zeros((seq, tq), F32)
    for h in range(IDX_HEADS):
        s = _dot_nt(ik, iq_ref[:, h * IDX_DIM:(h + 1) * IDX_DIM])
        sc = sc + jnp.maximum(s, 0.0) * (iwt_ref[h:h + 1, :] * w_scale)
    kpos = lax.broadcasted_iota(jnp.int32, (seq, tq), 0)
    tpos = t0 + lax.broadcasted_iota(jnp.int32, (seq, tq), 1)
    causal = kpos <= tpos
    sc = jnp.where(causal, sc + 0.0, NEG_INF)
    bits = pltpu.bitcast(sc, jnp.int32)
    key_ref[...] = bits ^ ((bits >> 31) & jnp.int32(0x7FFFFFFF))

    def bit_step(it, thr):
        cand = thr + jnp.left_shift(jnp.int32(1), 31 - it)
        cnt = jnp.sum((key_ref[...] >= cand).astype(F32), axis=0, keepdims=True)
        return jnp.where(cnt >= float(topk), cand, thr)

    thr = lax.fori_loop(0, 32, bit_step, jnp.full((1, tq), jnp.iinfo(jnp.int32).min, jnp.int32))
    addmask = jnp.where((key_ref[...] >= thr) & causal, 0.0, NEG_INF)

    wn = tq + REL_MAX_DIST
    for h in range(AT_HEADS):
        lg_ref[...] = _dot_nt(ckv_ref[...], qlat_ref[:, h * KV_RANK:(h + 1) * KV_RANK]) + addmask

        if tile_lo == 0:
            @pl.when(i == 0)
            def _():
                lg_ref[0:tq, :] += near_ref[h, REL_MAX_DIST:wn, :]

        @pl.when(i > 0)
        def _():
            start = pl.multiple_of(t0 - REL_MAX_DIST, REL_MAX_DIST)
            lg_ref[pl.ds(start, wn), :] += near_ref[h]

        lg = lg_ref[...]
        p = jnp.exp(lg - jnp.max(lg, axis=0, keepdims=True))
        denom = jnp.sum(p, axis=0, keepdims=True)
        olat = _dot(ckvt_ref[...], p.astype(BF16)) / denom
        out_t = _dot(wuvt_ref[h], olat.astype(BF16))
        o_ref[:, h * AT_DH:(h + 1) * AT_DH] = out_t.T.astype(o_ref.dtype)


def _near_bias_table(rel_bias, tq):
    wn = tq + REL_MAX_DIST
    d = (jnp.arange(tq, dtype=jnp.int32)[None, :] - jnp.arange(wn, dtype=jnp.int32)[:, None]
         + REL_MAX_DIST)
    max_exact = REL_BUCKETS // 2
    dd = jnp.maximum(d, 0)
    large = max_exact + (jnp.log(jnp.maximum(dd, 1).astype(F32) / max_exact)
                         / math.log(REL_MAX_DIST / max_exact) * (REL_BUCKETS - max_exact)).astype(jnp.int32)
    large = jnp.minimum(large, REL_BUCKETS - 1)
    bucket = jnp.where(dd < max_exact, dd, large)
    rb = rel_bias.astype(F32) - rel_bias.astype(F32)[REL_BUCKETS - 1]
    onehot = (bucket[None, :, :] == jnp.arange(REL_BUCKETS, dtype=jnp.int32)[:, None, None]).astype(F32)
    return jnp.einsum("bh,bcr->hcr", rb, onehot, precision=lax.Precision.HIGHEST)


def _dsa_group(p16, iw_t, ikn, qlat, ckvn, ckvn_t, w_uv_t, near, batch, seq, tq, tile_lo, n_tiles):
    nq = seq // tq
    klen = (tile_lo + n_tiles) * tq
    topk = min(TOPK_MAX, seq // 4)
    wn = tq + REL_MAX_DIST
    qrow = lambda b, i: b * nq + tile_lo + i
    return pl.pallas_call(
        functools.partial(_dsa_kernel, tq=tq, seq=klen, topk=topk, tile_lo=tile_lo),
        grid=(batch, n_tiles),
        in_specs=[pl.BlockSpec((tq, IDX_HEADS * IDX_DIM), lambda b, i: (qrow(b, i), 1)),
                  pl.BlockSpec((IDX_HEADS, tq), lambda b, i: (0, qrow(b, i))),
                  pl.BlockSpec((None, klen, IDX_DIM), lambda b, i: (b, 0, 0)),
                  pl.BlockSpec((tq, AT_HEADS * KV_RANK), lambda b, i: (qrow(b, i), 0)),
                  pl.BlockSpec((None, klen, KV_RANK), lambda b, i: (b, 0, 0)),
                  pl.BlockSpec((None, KV_RANK, klen), lambda b, i: (b, 0, 0)),
                  pl.BlockSpec((AT_HEADS, AT_DH, KV_RANK), lambda b, i: (0, 0, 0)),
                  pl.BlockSpec((AT_HEADS, wn, tq), lambda b, i: (0, 0, 0))],
        out_specs=pl.BlockSpec((None, tq, AT_WIDTH), lambda b, i: (b, i, 0)),
        out_shape=jax.ShapeDtypeStruct((batch, n_tiles * tq, AT_WIDTH), BF16),
        scratch_shapes=[pltpu.VMEM((klen, tq), jnp.int32), pltpu.VMEM((klen, tq), F32)],
        compiler_params=_cparams(("parallel", "arbitrary")),
    )(p16, iw_t, ikn.reshape(batch, seq, IDX_DIM), qlat, ckvn.reshape(batch, seq, KV_RANK),
      ckvn_t, w_uv_t, near)


def _dsa(p16, iw_t, ikn, qlat, ckvn, ckvn_t, w_uv_t, near, batch, seq, tq, n_groups=4):
    per = seq // tq // n_groups
    outs = [_dsa_group(p16, iw_t, ikn, qlat, ckvn, ckvn_t, w_uv_t, near, batch, seq, tq, g * per, per)
            for g in range(n_groups)]
    return jnp.concatenate(outs, axis=1).reshape(batch * seq, AT_WIDTH)


def _xattn_kernel(q_ref, k_ref, v_ref, o_ref):
    scale = MEM_DH ** -0.5
    for h in range(MEM_HEADS):
        cols = slice(h * MEM_DH, (h + 1) * MEM_DH)
        lg = _dot_nt(q_ref[:, cols], k_ref[:, cols]) * scale
        p = jnp.exp(lg - jnp.max(lg, axis=-1, keepdims=True))
        p = p / jnp.sum(p, axis=-1, keepdims=True)
        o_ref[:, cols] = _dot(p.astype(BF16), v_ref[:, cols]).astype(o_ref.dtype)


def _xattn(q, k, v, batch, seq, mlen, tq):
    t = batch * seq
    nq = seq // tq
    return pl.pallas_call(
        _xattn_kernel,
        grid=(batch, nq),
        in_specs=[pl.BlockSpec((tq, D_MODEL), lambda b, i: (b * nq + i, 0)),
                  pl.BlockSpec((mlen, D_MODEL), lambda b, i: (b, 0)),
                  pl.BlockSpec((mlen, D_MODEL), lambda b, i: (b, 0))],
        out_specs=pl.BlockSpec((tq, D_MODEL), lambda b, i: (b * nq + i, 0)),
        out_shape=jax.ShapeDtypeStruct((t, D_MODEL), BF16),
        compiler_params=_cparams(("parallel", "parallel")),
    )(q, k, v)


def _top_values(x, k):
    rows = lax.broadcasted_iota(jnp.int32, (k, x.shape[1]), 0)
    out = jnp.zeros((k, x.shape[1]), F32)
    for a in range(k):
        mx = jnp.max(x, axis=0, keepdims=True)
        out = jnp.where(rows == a, mx, out)
        x = jnp.where(x == mx, NEG_INF, x)
    return out


def _peer_kernel(qp_ref, keys_ref, xt_ref, u_ref, vt_ref, res_ref, g_ref, b_ref, o_ref,
                 s_ref, e_ref, th_ref, acc_ref, wg_ref, *, tt, te, tc):
    e = pl.program_id(1)
    nk = PEER_NKEYS
    k = PEER_TOPK

    @pl.when(e == 0)
    def _():
        acc_ref[...] = jnp.zeros_like(acc_ref)
        for hp in range(2 * PEER_HEADS):
            s_ref[hp] = _dot_nt(keys_ref[hp], qp_ref[:, hp * nk:(hp + 1) * nk])
        for h in range(PEER_HEADS):
            s0 = s_ref[2 * h]
            s1 = s_ref[2 * h + 1]
            v0 = _top_values(s0, k)
            v1 = _top_values(s1, k)
            cand = jnp.concatenate([v0[a:a + 1, :] + v1 for a in range(k)], axis=0)
            top = _top_values(cand, k)
            z = jnp.sum(jnp.exp(top - top[0:1, :]), axis=0, keepdims=True)
            th_ref[h:h + 1, :] = top[k - 1:k, :]
            e_ref[2 * h] = jnp.exp(s0 - v0[0:1, :]) / z
            e_ref[2 * h + 1] = jnp.exp(s1 - v1[0:1, :])

    hts = [_dot(u_ref[c * tc:(c + 1) * tc, :], xt_ref[...]) for c in range(te // tc)]
    acc = acc_ref[...]
    for c in range(te // tc):
        rows = slice(c * tc, (c + 1) * tc)
        ht = hts[c]
        gel = 0.5 * ht * (1.0 + lax.erf(ht * (2.0 ** -0.5)))
        for ii in range(tc // nk):
            irow = e * (te // nk) + c * (tc // nk) + ii
            sub = slice(ii * nk, (ii + 1) * nk)
            s0rows = [s_ref[2 * h, pl.ds(irow, 1), :] for h in range(PEER_HEADS)]
            e0rows = [e_ref[2 * h, pl.ds(irow, 1), :] for h in range(PEER_HEADS)]
            for lc in range(tt // LANES):
                tok = slice(lc * LANES, (lc + 1) * LANES)
                w = jnp.zeros((nk, LANES), F32)
                for h in range(PEER_HEADS):
                    ssum = s0rows[h][:, tok] + s_ref[2 * h + 1, :, tok]
                    w = w + jnp.where(ssum >= th_ref[h:h + 1, tok],
                                      e0rows[h][:, tok] * e_ref[2 * h + 1, :, tok], 0.0)
                wg_ref[c * tc + ii * nk:c * tc + (ii + 1) * nk, tok] = (w * gel[sub, tok]).astype(BF16)
        acc = acc + _dot(vt_ref[:, rows], wg_ref[rows, :])
    acc_ref[...] = acc

    @pl.when(e == pl.num_programs(1) - 1)
    def _():
        for c in range(tt // tc):
            tok = slice(c * tc, (c + 1) * tc)
            y = ALPHA * res_ref[tok, :] + acc_ref[:, tok].T
            o_ref[tok, :] = _layer_norm_rows(y, g_ref[...], b_ref[...])


def _peer(qp, keys, h2, u, v_t, g, b, tt, te, tc):
    t, d = h2.shape
    n = u.shape[0]
    return pl.pallas_call(
        functools.partial(_peer_kernel, tt=tt, te=te, tc=tc),
        grid=(t // tt, n // te),
        in_specs=[pl.BlockSpec((tt, PEER_HEADS * PEER_DKEY), lambda i, e: (i, 0)),
                  pl.BlockSpec((2 * PEER_HEADS, PEER_NKEYS, PEER_DKEY // 2), lambda i, e: (0, 0, 0)),
                  pl.BlockSpec((d, tt), lambda i, e: (0, i)),
                  pl.BlockSpec((te, d), lambda i, e: (e, 0)),
                  pl.BlockSpec((d, te), lambda i, e: (0, e)),
                  pl.BlockSpec((tt, d), lambda i, e: (i, 0)),
                  pl.BlockSpec((1, d), lambda i, e: (0, 0)),
                  pl.BlockSpec((1, d), lambda i, e: (0, 0))],
        out_specs=pl.BlockSpec((tt, d), lambda i, e: (i, 0)),
        out_shape=jax.ShapeDtypeStruct((t, d), F32),
        scratch_shapes=[pltpu.VMEM((2 * PEER_HEADS, PEER_NKEYS, tt), F32),
                        pltpu.VMEM((2 * PEER_HEADS, PEER_NKEYS, tt), F32),
                        pltpu.VMEM((PEER_HEADS, tt), F32),
                        pltpu.VMEM((d, tt), F32),
                        pltpu.VMEM((te, tt), BF16)],
        compiler_params=_cparams(("parallel", "arbitrary")),
    )(qp, keys, h2.astype(BF16).T, u, v_t, h2, g.reshape(1, d), b.reshape(1, d))


def kernel(x, mem, w_in, lb_logits, hg_norm_g, kv_norm_g, ik_norm_g, ik_norm_b, w_uk, w_uv,
           rel_bias, w_branch_a, w_branch_b, w_mix_out, ln1_g, ln1_b, w_cq, w_ck, w_cv, w_co,
           ln2_g, ln2_b, w_pq, peer_keys, peer_u, peer_v, ln3_g, ln3_b):
    batch, seq, d = x.shape
    mlen = mem.shape[1]
    t = batch * seq
    l = 0

    wi = w_in[l]
    o_aq = 4 * HG_WIDTH
    o_ckv = o_aq + AT_WIDTH
    o_iq = o_ckv + KV_RANK
    o_ik = o_iq + IDX_HEADS * IDX_DIM
    o_iw = o_ik + IDX_DIM
    o_ga = o_iw + IDX_HEADS
    w32 = jnp.concatenate([wi[:, :o_aq], wi[:, o_ckv:o_iq], wi[:, o_ik:o_iw],
                           jnp.pad(wi[:, o_iw:o_ga], ((0, 0), (0, LANES - IDX_HEADS))),
                           wi[:, o_ga:]], axis=1).astype(BF16)
    w16 = jnp.concatenate([wi[:, o_aq:o_ckv], wi[:, o_iq:o_ik]], axis=1).astype(BF16)
    xf = x.reshape(t, d)
    xb = xf.astype(BF16)

    p32 = _matmul(xb, w32, 2048, 512, F32)
    p16 = _matmul(xb, w16, 2048, 1024, BF16)
    ya_in = _hgrn(p32, lb_logits, hg_norm_g[l], batch, seq)
    ckvn, ikn, qlat = _dsa_prep(p32, p16, kv_norm_g[l], ik_norm_g[l], ik_norm_b[l],
                                w_uk[l].astype(BF16), 1024)
    tq = 256
    iw_t = p32[:, C_IW:C_IW + IDX_HEADS].T
    ckvn_t = jnp.swapaxes(ckvn.reshape(batch, seq, KV_RANK), 1, 2)
    w_uv_t = jnp.swapaxes(w_uv[l], 1, 2).astype(BF16)
    near = _near_bias_table(rel_bias, tq)
    yb_in = _dsa(p16, iw_t, ikn, qlat, ckvn, ckvn_t, w_uv_t, near, batch, seq, tq)
    mix_in = _merge(ya_in, yb_in, w_branch_a[l].astype(BF16), w_branch_b[l].astype(BF16), p32, 1024, 512)
    h1 = _matmul_res_ln(mix_in, w_mix_out[l].astype(BF16), xf, ln1_g[l], ln1_b[l], 512)

    qc = _matmul(h1, w_cq[l].astype(BF16), 1024, 1024, BF16)
    memf = mem.reshape(batch * mlen, d).astype(BF16)
    kc = _matmul(memf, w_ck[l].astype(BF16), 1024, 1024, BF16)
    vc = _matmul(memf, w_cv[l].astype(BF16), 1024, 1024, BF16)
    ca_in = _xattn(qc, kc, vc, batch, seq, mlen, 512)
    h2 = _matmul_res_ln(ca_in, w_co[l].astype(BF16), h1, ln2_g[l], ln2_b[l], 512)

    qp = _matmul(h2, w_pq[l].astype(BF16), 1024, 1024, BF16)
    keys = peer_keys[l].reshape(2 * PEER_HEADS, PEER_NKEYS, PEER_DKEY // 2).astype(BF16)
    u = peer_u[l].astype(BF16)
    v_t = peer_v[l].astype(BF16).T
    h3 = _peer(qp, keys, h2, u, v_t, ln3_g[l], ln3_b[l], 512, 512, 256)
    return h3.reshape(batch, seq, d)
```

```python
import functools
import math

import jax
import jax.numpy as jnp
from jax import lax
from jax.experimental import pallas as pl
from jax.experimental.pallas import tpu as pltpu

F32 = jnp.float32
BF16 = jnp.bfloat16

D_MODEL = 2048
ALPHA = 2.0 ** 0.25
LN_EPS = 1e-5
RMS_EPS = 1e-6
HG_DK = 128
HG_HEADS = D_MODEL // 128
HG_WIDTH = HG_HEADS * HG_DK
HG_CHUNK = 64
AT_DH = 128
AT_HEADS = D_MODEL // 128
AT_WIDTH = AT_HEADS * AT_DH
KV_RANK = D_MODEL // 8
IDX_HEADS = 16
IDX_DIM = 128
TOPK_MAX = 256
REL_BUCKETS = 32
REL_MAX_DIST = 128
MEM_HEADS = 4
MEM_DH = D_MODEL // MEM_HEADS
PEER_HEADS = 8
PEER_NKEYS = 128
PEER_N = PEER_NKEYS * PEER_NKEYS
PEER_DKEY = 256
PEER_TOPK = 16

LANES = 128
VMEM_LIMIT = 56 * 1024 * 1024

C_HQ, C_HF, C_HI, C_HG = 0, HG_WIDTH, 2 * HG_WIDTH, 3 * HG_WIDTH
C_CKV = 4 * HG_WIDTH
C_IK = C_CKV + KV_RANK
C_IW = C_IK + IDX_DIM
C_GA = C_IW + LANES
C_GB = C_GA + D_MODEL
N_P32 = C_GB + D_MODEL

NEG_INF = float("-inf")


def _cparams(sem):
    return pltpu.CompilerParams(dimension_semantics=sem, vmem_limit_bytes=VMEM_LIMIT)


def _dot(a, b):
    return jnp.dot(a, b, preferred_element_type=F32)


def _dot_nt(a, b):
    return lax.dot_general(a, b, (((1,), (1,)), ((), ())), preferred_element_type=F32)


def _dot_tn(a, b):
    return lax.dot_general(a, b, (((0,), (0,)), ((), ())), preferred_element_type=F32)


def _mm_kernel(a_ref, w_ref, o_ref):
    o_ref[...] = _dot(a_ref[...].astype(BF16), w_ref[...]).astype(o_ref.dtype)


def _matmul(a, w, tm, tn, out_dtype):
    m, k = a.shape
    n = w.shape[1]
    return pl.pallas_call(
        _mm_kernel,
        grid=(m // tm, n // tn),
        in_specs=[pl.BlockSpec((tm, k), lambda i, j: (i, 0)),
                  pl.BlockSpec((k, tn), lambda i, j: (0, j))],
        out_specs=pl.BlockSpec((tm, tn), lambda i, j: (i, j)),
        out_shape=jax.ShapeDtypeStruct((m, n), out_dtype),
        compiler_params=_cparams(("parallel", "parallel")),
    )(a, w)


def _layer_norm_rows(y, g, b):
    mu = jnp.mean(y, axis=-1, keepdims=True)
    yc = y - mu
    var = jnp.mean(yc * yc, axis=-1, keepdims=True)
    return yc * lax.rsqrt(var + LN_EPS) * g + b


def _mm_res_ln_kernel(a_ref, w_ref, res_ref, g_ref, b_ref, o_ref):
    y = ALPHA * res_ref[...] + _dot(a_ref[...].astype(BF16), w_ref[...])
    o_ref[...] = _layer_norm_rows(y, g_ref[...], b_ref[...])


def _matmul_res_ln(a, w, res, g, b, tm):
    m, k = a.shape
    n = w.shape[1]
    return pl.pallas_call(
        _mm_res_ln_kernel,
        grid=(m // tm,),
        in_specs=[pl.BlockSpec((tm, k), lambda i: (i, 0)),
                  pl.BlockSpec((k, n), lambda i: (0, 0)),
                  pl.BlockSpec((tm, n), lambda i: (i, 0)),
                  pl.BlockSpec((1, n), lambda i: (0, 0)),
                  pl.BlockSpec((1, n), lambda i: (0, 0))],
        out_specs=pl.BlockSpec((tm, n), lambda i: (i, 0)),
        out_shape=jax.ShapeDtypeStruct((m, n), F32),
        compiler_params=_cparams(("parallel",)),
    )(a, w, res, g.reshape(1, n), b.reshape(1, n))


def _merge_kernel(a1_ref, a2_ref, w1_ref, w2_ref, ga_ref, gb_ref, o_ref):
    ya = _dot(a1_ref[...], w1_ref[...])
    yb = _dot(a2_ref[...], w2_ref[...])
    o_ref[...] = (jax.nn.sigmoid(ga_ref[...]) * ya + jax.nn.sigmoid(gb_ref[...]) * yb).astype(o_ref.dtype)


def _merge(a1, a2, w1, w2, p32, tm, tn):
    m, k = a1.shape
    n = w1.shape[1]
    ga_blk, gb_blk = C_GA // tn, C_GB // tn
    return pl.pallas_call(
        _merge_kernel,
        grid=(m // tm, n // tn),
        in_specs=[pl.BlockSpec((tm, k), lambda i, j: (i, 0)),
                  pl.BlockSpec((tm, k), lambda i, j: (i, 0)),
                  pl.BlockSpec((k, tn), lambda i, j: (0, j)),
                  pl.BlockSpec((k, tn), lambda i, j: (0, j)),
                  pl.BlockSpec((tm, tn), lambda i, j: (i, ga_blk + j)),
                  pl.BlockSpec((tm, tn), lambda i, j: (i, gb_blk + j))],
        out_specs=pl.BlockSpec((tm, tn), lambda i, j: (i, j)),
        out_shape=jax.ShapeDtypeStruct((m, n), BF16),
        compiler_params=_cparams(("parallel", "parallel")),
    )(a1, a2, w1, w2, p32, p32)


def _hgrn_kernel(q_ref, f_ref, i_ref, g_ref, lb_ref, ng_ref, o_ref, st_ref, *, hb, rb):
    c = HG_CHUNK

    @pl.when(pl.program_id(2) == 0)
    def _():
        st_ref[...] = jnp.zeros_like(st_ref)

    l0 = lb_ref[0:1, :]
    l1 = lb_ref[1:2, :]
    mx = jnp.maximum(l0, l1)
    e0 = jnp.exp(l0 - mx)
    lb = e0 / (e0 + jnp.exp(l1 - mx))

    f = lb + (1.0 - lb) * jax.nn.sigmoid(f_ref[...])
    kk = 1.0 - f
    gcum = jnp.log(f)
    pos = lax.broadcasted_iota(jnp.int32, gcum.shape, 0) & (c - 1)
    for s in (1, 2, 4, 8, 16, 32):
        gcum = gcum + jnp.where(pos >= s, pltpu.roll(gcum, s, axis=0), 0.0)

    tril = (lax.broadcasted_iota(jnp.int32, (c, c), 0) >= lax.broadcasted_iota(jnp.int32, (c, c), 1))
    ng = ng_ref[...]
    for ci in range(rb // c):
        rows = slice(ci * c, (ci + 1) * c)
        for h in range(hb):
            cols = slice(h * HG_DK, (h + 1) * HG_DK)
            gc = gcum[rows, cols]
            g_ref_row = gc[c // 2 - 1:c // 2, :]
            g_last = gc[c - 1:c, :]
            qc = q_ref[rows, cols]
            kc = kk[rows, cols]
            vc = i_ref[rows, cols].astype(BF16)
            qa = (qc * jnp.exp(gc - g_ref_row)).astype(BF16)
            kb = (kc * jnp.exp(g_ref_row - gc)).astype(BF16)
            a = jnp.where(tril, _dot_nt(qa, kb), 0.0)
            o = _dot(a.astype(BF16), vc)
            st = st_ref[h]
            qd = (qc * jnp.exp(gc)).astype(BF16)
            o = o + _dot_nt(qd, st.astype(BF16))
            kd = (kc * jnp.exp(g_last - gc)).astype(BF16)
            st_ref[h] = st * jnp.exp(g_last) + _dot_tn(vc, kd)
            o = o * lax.rsqrt(jnp.mean(o * o, axis=-1, keepdims=True) + RMS_EPS) * ng[:, cols]
            gate = g_ref[rows, cols]
            o_ref[rows, cols] = (o * (gate * jax.nn.sigmoid(gate))).astype(o_ref.dtype)


def _hgrn(p32, lb_logits, norm_g, batch, seq, hb=4, rb=512):
    t = batch * seq
    w = hb * HG_DK
    nr = seq // rb
    hq, hf, hi, hg = (cc // w for cc in (C_HQ, C_HF, C_HI, C_HG))
    row = lambda b, h, r: b * nr + r
    in_spec = lambda off: pl.BlockSpec((rb, w), lambda b, h, r: (row(b, h, r), off + h))
    return pl.pallas_call(
        functools.partial(_hgrn_kernel, hb=hb, rb=rb),
        grid=(batch, HG_HEADS // hb, nr),
        in_specs=[in_spec(hq), in_spec(hf), in_spec(hi), in_spec(hg),
                  pl.BlockSpec((2, w), lambda b, h, r: (0, h)),
                  pl.BlockSpec((1, w), lambda b, h, r: (0, h))],
        out_specs=pl.BlockSpec((rb, w), lambda b, h, r: (row(b, h, r), h)),
        out_shape=jax.ShapeDtypeStruct((t, HG_WIDTH), BF16),
        scratch_shapes=[pltpu.VMEM((hb, HG_DK, HG_DK), F32)],
        compiler_params=_cparams(("parallel", "parallel", "arbitrary")),
    )(p32, p32, p32, p32, lb_logits, norm_g.reshape(1, HG_WIDTH))


def _dsa_prep_kernel(ckv_ref, ik_ref, aq_ref, kvg_ref, ikg_ref, ikb_ref, wuk_ref,
                     ckvn_ref, ikn_ref, qlat_ref):
    ckv = ckv_ref[...]
    ckvn_ref[...] = (ckv * lax.rsqrt(jnp.mean(ckv * ckv, axis=-1, keepdims=True) + RMS_EPS)
                     * kvg_ref[...]).astype(ckvn_ref.dtype)
    ikn_ref[...] = _layer_norm_rows(ik_ref[...], ikg_ref[...], ikb_ref[...]).astype(ikn_ref.dtype)
    scale = AT_DH ** -0.5
    for h in range(AT_HEADS):
        qh = aq_ref[:, h * AT_DH:(h + 1) * AT_DH]
        qlat_ref[:, h * KV_RANK:(h + 1) * KV_RANK] = (_dot_nt(qh, wuk_ref[h]) * scale).astype(qlat_ref.dtype)


def _dsa_prep(p32, p16, kv_g, ik_g, ik_b, w_uk, tm):
    t = p32.shape[0]
    return pl.pallas_call(
        _dsa_prep_kernel,
        grid=(t // tm,),
        in_specs=[pl.BlockSpec((tm, KV_RANK), lambda i: (i, C_CKV // KV_RANK)),
                  pl.BlockSpec((tm, IDX_DIM), lambda i: (i, C_IK // IDX_DIM)),
                  pl.BlockSpec((tm, AT_WIDTH), lambda i: (i, 0)),
                  pl.BlockSpec((1, KV_RANK), lambda i: (0, 0)),
                  pl.BlockSpec((1, IDX_DIM), lambda i: (0, 0)),
                  pl.BlockSpec((1, IDX_DIM), lambda i: (0, 0)),
                  pl.BlockSpec((AT_HEADS, KV_RANK, AT_DH), lambda i: (0, 0, 0))],
        out_specs=[pl.BlockSpec((tm, KV_RANK), lambda i: (i, 0)),
                   pl.BlockSpec((tm, IDX_DIM), lambda i: (i, 0)),
                   pl.BlockSpec((tm, AT_HEADS * KV_RANK), lambda i: (i, 0))],
        out_shape=[jax.ShapeDtypeStruct((t, KV_RANK), BF16),
                   jax.ShapeDtypeStruct((t, IDX_DIM), BF16),
                   jax.ShapeDtypeStruct((t, AT_HEADS * KV_RANK), BF16)],
        compiler_params=_cparams(("parallel",)),
    )(p32, p32, p16, kv_g.reshape(1, -1), ik_g.reshape(1, -1), ik_b.reshape(1, -1), w_uk)


def _dsa_kernel(iq_ref, iwt_ref, ik_ref, qlat_ref, ckv_ref, ckvt_ref, wuvt_ref, near_ref,
                o_ref, key_ref, lg_ref, *, tq, seq, topk, tile_lo):
    i = tile_lo + pl.program_id(1)
    t0 = i * tq
    ik = ik_ref[...]
    w_scale = (IDX_DIM ** -0.5) * (IDX_HEADS ** -0.5)
    sc = jnp.zeros((seq, tq), F32)
    for h in range(IDX_HEADS):
        s = _dot_nt(ik, iq_ref[:, h * IDX_DIM:(h + 1) * IDX_DIM])
        sc = sc + jnp.maximum(s, 0.0) * (iwt_ref[h:h + 1, :] * w_scale)
    kpos = lax.broadcasted_iota(jnp.int32, (seq, tq), 0)
    tpos = t0 + lax.broadcasted_iota(jnp.int32, (seq, tq), 1)
    causal = kpos <= tpos
    sc = jnp.where(causal, sc + 0.0, NEG_INF)
    bits = pltpu.bitcast(sc, jnp.int32)
    key_ref[...] = bits ^ ((bits >> 31) & jnp.int32(0x7FFFFFFF))

    def bit_step(it, thr):
        cand = thr + jnp.left_shift(jnp.int32(1), 31 - it)
        cnt = jnp.sum((key_ref[...] >= cand).astype(F32), axis=0, keepdims=True)
        return jnp.where(cnt >= float(topk), cand, thr)

    thr = lax.fori_loop(0, 32, bit_step, jnp.full((1, tq), jnp.iinfo(jnp.int32).min, jnp.int32))
    addmask = jnp.where((key_ref[...] >= thr) & causal, 0.0, NEG_INF)

    wn = tq + REL_MAX_DIST
    for h in range(AT_HEADS):
        lg_h = lg_ref.at[h % 2]
        lg_h[...] = _dot_nt(ckv_ref[...], qlat_ref[:, h * KV_RANK:(h + 1) * KV_RANK]) + addmask
        if tile_lo == 0:
            lg_h[0:tq, :] += near_ref[h, REL_MAX_DIST:wn, :]
        else:
            start = pl.multiple_of(t0 - REL_MAX_DIST, REL_MAX_DIST)
            lg_h[pl.ds(start, wn), :] += near_ref[h]
        lg = lg_h[...]
        p = jnp.exp(lg - jnp.max(lg, axis=0, keepdims=True))
        denom = jnp.sum(p, axis=0, keepdims=True)
        olat = _dot(ckvt_ref[...], p.astype(BF16)) / denom
        out_t = _dot(wuvt_ref[h], olat.astype(BF16))
        o_ref[:, h * AT_DH:(h + 1) * AT_DH] = out_t.T.astype(o_ref.dtype)


def _near_bias_table(rel_bias, tq):
    wn = tq + REL_MAX_DIST
    d = (jnp.arange(tq, dtype=jnp.int32)[None, :] - jnp.arange(wn, dtype=jnp.int32)[:, None]
         + REL_MAX_DIST)
    max_exact = REL_BUCKETS // 2
    dd = jnp.maximum(d, 0)
    large = max_exact + (jnp.log(jnp.maximum(dd, 1).astype(F32) / max_exact)
                         / math.log(REL_MAX_DIST / max_exact) * (REL_BUCKETS - max_exact)).astype(jnp.int32)
    large = jnp.minimum(large, REL_BUCKETS - 1)
    bucket = jnp.where(dd < max_exact, dd, large)
    rb = rel_bias.astype(F32) - rel_bias.astype(F32)[REL_BUCKETS - 1]
    onehot = (bucket[None, :, :] == jnp.arange(REL_BUCKETS, dtype=jnp.int32)[:, None, None]).astype(F32)
    return jnp.einsum("bh,bcr->hcr", rb, onehot, precision=lax.Precision.HIGHEST)


def _dsa_group(p16, iw_t, ikn, qlat, ckvn, ckvn_t, w_uv_t, near, batch, seq, tq, tile_lo, n_tiles):
    nq = seq // tq
    klen = (tile_lo + n_tiles) * tq
    topk = min(TOPK_MAX, seq // 4)
    wn = tq + REL_MAX_DIST
    qrow = lambda b, i: b * nq + tile_lo + i
    return pl.pallas_call(
        functools.partial(_dsa_kernel, tq=tq, seq=klen, topk=topk, tile_lo=tile_lo),
        grid=(batch, n_tiles),
        in_specs=[pl.BlockSpec((tq, IDX_HEADS * IDX_DIM), lambda b, i: (qrow(b, i), 1)),
                  pl.BlockSpec((IDX_HEADS, tq), lambda b, i: (0, qrow(b, i))),
                  pl.BlockSpec((None, klen, IDX_DIM), lambda b, i: (b, 0, 0)),
                  pl.BlockSpec((tq, AT_HEADS * KV_RANK), lambda b, i: (qrow(b, i), 0)),
                  pl.BlockSpec((None, klen, KV_RANK), lambda b, i: (b, 0, 0)),
                  pl.BlockSpec((None, KV_RANK, klen), lambda b, i: (b, 0, 0)),
                  pl.BlockSpec((AT_HEADS, AT_DH, KV_RANK), lambda b, i: (0, 0, 0)),
                  pl.BlockSpec((AT_HEADS, wn, tq), lambda b, i: (0, 0, 0))],
        out_specs=pl.BlockSpec((None, tq, AT_WIDTH), lambda b, i: (b, i, 0)),
        out_shape=jax.ShapeDtypeStruct((batch, n_tiles * tq, AT_WIDTH), BF16),
        scratch_shapes=[pltpu.VMEM((klen, tq), jnp.int32), pltpu.VMEM((2, klen, tq), F32)],
        compiler_params=_cparams(("parallel", "arbitrary")),
    )(p16, iw_t, ikn.reshape(batch, seq, IDX_DIM), qlat, ckvn.reshape(batch, seq, KV_RANK),
      ckvn_t, w_uv_t, near)


def _dsa(p16, iw_t, ikn, qlat, ckvn, ckvn_t, w_uv_t, near, batch, seq, tq):
    nq = seq // tq
    bounds = [0, 1] + list(range(2, nq + 1, 2))
    if bounds[-1] != nq:
        bounds.append(nq)
    outs = [_dsa_group(p16, iw_t, ikn, qlat, ckvn, ckvn_t, w_uv_t, near, batch, seq, tq, lo, hi - lo)
            for lo, hi in zip(bounds[:-1], bounds[1:])]
    return jnp.concatenate(outs, axis=1).reshape(batch * seq, AT_WIDTH)


def _xattn_kernel(q_ref, k_ref, v_ref, o_ref):
    scale = MEM_DH ** -0.5
    for h in range(MEM_HEADS):
        cols = slice(h * MEM_DH, (h + 1) * MEM_DH)
        lg = _dot_nt(q_ref[:, cols], k_ref[:, cols]) * scale
        p = jnp.exp(lg - jnp.max(lg, axis=-1, keepdims=True))
        p = p / jnp.sum(p, axis=-1, keepdims=True)
        o_ref[:, cols] = _dot(p.astype(BF16), v_ref[:, cols]).astype(o_ref.dtype)


def _xattn(q, k, v, batch, seq, mlen, tq):
    t = batch * seq
    nq = seq // tq
    return pl.pallas_call(
        _xattn_kernel,
        grid=(batch, nq),
        in_specs=[pl.BlockSpec((tq, D_MODEL), lambda b, i: (b * nq + i, 0)),
                  pl.BlockSpec((mlen, D_MODEL), lambda b, i: (b, 0)),
                  pl.BlockSpec((mlen, D_MODEL), lambda b, i: (b, 0))],
        out_specs=pl.BlockSpec((tq, D_MODEL), lambda b, i: (b * nq + i, 0)),
        out_shape=jax.ShapeDtypeStruct((t, D_MODEL), BF16),
        compiler_params=_cparams(("parallel", "parallel")),
    )(q, k, v)


def _top_values(x, k):
    rows = lax.broadcasted_iota(jnp.int32, (k, x.shape[1]), 0)
    out = jnp.zeros((k, x.shape[1]), F32)
    for a in range(k):
        mx = jnp.max(x, axis=0, keepdims=True)
        out = jnp.where(rows == a, mx, out)
        x = jnp.where(x == mx, NEG_INF, x)
    return out


def _peer_kernel(qp_ref, keys_ref, xt_ref, u_ref, vt_ref, res_ref, g_ref, b_ref, o_ref,
                 s_ref, e_ref, th_ref, acc_ref, wg_ref, *, tt, te, tc):
    e = pl.program_id(1)
    nk = PEER_NKEYS
    k = PEER_TOPK

    @pl.when(e == 0)
    def _():
        acc_ref[...] = jnp.zeros_like(acc_ref)
        for hp in range(2 * PEER_HEADS):
            s_ref[hp] = _dot_nt(keys_ref[hp], qp_ref[:, hp * nk:(hp + 1) * nk])
        for h in range(PEER_HEADS):
            s0 = s_ref[2 * h]
            s1 = s_ref[2 * h + 1]
            v0 = _top_values(s0, k)
            v1 = _top_values(s1, k)
            half = k // 2
            cand = jnp.concatenate(
                [v0[0:1, :] + v1]
                + [v0[a:a + 1, :] + v1[0:half, :] for a in range(1, half)]
                + [v0[half:k, :] + v1[0:1, :]], axis=0)
            top = _top_values(cand, k)
            z = jnp.sum(jnp.exp(top - top[0:1, :]), axis=0, keepdims=True)
            th_ref[h:h + 1, :] = top[k - 1:k, :]
            e_ref[2 * h] = jnp.exp(s0 - v0[0:1, :]) / z
            e_ref[2 * h + 1] = jnp.exp(s1 - v1[0:1, :])

    hts = [_dot(u_ref[c * tc:(c + 1) * tc, :], xt_ref[...]) for c in range(te // tc)]
    acc = acc_ref[...]
    for c in range(te // tc):
        rows = slice(c * tc, (c + 1) * tc)
        ht = hts[c]
        gel = 0.5 * ht * (1.0 + lax.erf(ht * (2.0 ** -0.5)))
        for ii in range(tc // nk):
            irow = e * (te // nk) + c * (tc // nk) + ii
            sub = slice(ii * nk, (ii + 1) * nk)
            s0rows = [s_ref[2 * h, pl.ds(irow, 1), :] for h in range(PEER_HEADS)]
            e0rows = [e_ref[2 * h, pl.ds(irow, 1), :] for h in range(PEER_HEADS)]
            for lc in range(tt // LANES):
                tok = slice(lc * LANES, (lc + 1) * LANES)
                w = jnp.zeros((nk, LANES), F32)
                for h in range(PEER_HEADS):
                    ssum = s0rows[h][:, tok] + s_ref[2 * h + 1, :, tok]
                    w = w + jnp.where(ssum >= th_ref[h:h + 1, tok],
                                      e0rows[h][:, tok] * e_ref[2 * h + 1, :, tok], 0.0)
                wg_ref[c * tc + ii * nk:c * tc + (ii + 1) * nk, tok] = (w * gel[sub, tok]).astype(BF16)
        acc = acc + _dot(vt_ref[:, rows], wg_ref[rows, :])
    acc_ref[...] = acc

    @pl.when(e == pl.num_programs(1) - 1)
    def _():
        for c in range(tt // tc):
            tok = slice(c * tc, (c + 1) * tc)
            y = ALPHA * res_ref[tok, :] + acc_ref[:, tok].T
            o_ref[tok, :] = _layer_norm_rows(y, g_ref[...], b_ref[...])


def _peer(qp, keys, h2, u, v_t, g, b, tt, te, tc):
    t, d = h2.shape
    n = u.shape[0]
    return pl.pallas_call(
        functools.partial(_peer_kernel, tt=tt, te=te, tc=tc),
        grid=(t // tt, n // te),
        in_specs=[pl.BlockSpec((tt, PEER_HEADS * PEER_DKEY), lambda i, e: (i, 0)),
                  pl.BlockSpec((2 * PEER_HEADS, PEER_NKEYS, PEER_DKEY // 2), lambda i, e: (0, 0, 0)),
                  pl.BlockSpec((d, tt), lambda i, e: (0, i)),
                  pl.BlockSpec((te, d), lambda i, e: (e, 0)),
                  pl.BlockSpec((d, te), lambda i, e: (0, e)),
                  pl.BlockSpec((tt, d), lambda i, e: (i, 0)),
                  pl.BlockSpec((1, d), lambda i, e: (0, 0)),
                  pl.BlockSpec((1, d), lambda i, e: (0, 0))],
        out_specs=pl.BlockSpec((tt, d), lambda i, e: (i, 0)),
        out_shape=jax.ShapeDtypeStruct((t, d), F32),
        scratch_shapes=[pltpu.VMEM((2 * PEER_HEADS, PEER_NKEYS, tt), F32),
                        pltpu.VMEM((2 * PEER_HEADS, PEER_NKEYS, tt), F32),
                        pltpu.VMEM((PEER_HEADS, tt), F32),
                        pltpu.VMEM((d, tt), F32),
                        pltpu.VMEM((te, tt), BF16)],
        compiler_params=_cparams(("parallel", "arbitrary")),
    )(qp, keys, h2.astype(BF16).T, u, v_t, h2, g.reshape(1, d), b.reshape(1, d))


def kernel(x, mem, w_in, lb_logits, hg_norm_g, kv_norm_g, ik_norm_g, ik_norm_b, w_uk, w_uv,
           rel_bias, w_branch_a, w_branch_b, w_mix_out, ln1_g, ln1_b, w_cq, w_ck, w_cv, w_co,
           ln2_g, ln2_b, w_pq, peer_keys, peer_u, peer_v, ln3_g, ln3_b):
    batch, seq, d = x.shape
    mlen = mem.shape[1]
    t = batch * seq
    l = 0

    wi = w_in[l]
    o_aq = 4 * HG_WIDTH
    o_ckv = o_aq + AT_WIDTH
    o_iq = o_ckv + KV_RANK
    o_ik = o_iq + IDX_HEADS * IDX_DIM
    o_iw = o_ik + IDX_DIM
    o_ga = o_iw + IDX_HEADS
    w32 = jnp.concatenate([wi[:, :o_aq], wi[:, o_ckv:o_iq], wi[:, o_ik:o_iw],
                           jnp.pad(wi[:, o_iw:o_ga], ((0, 0), (0, LANES - IDX_HEADS))),
                           wi[:, o_ga:]], axis=1).astype(BF16)
    w16 = jnp.concatenate([wi[:, o_aq:o_ckv], wi[:, o_iq:o_ik]], axis=1).astype(BF16)
    xf = x.reshape(t, d)
    xb = xf.astype(BF16)

    p32 = _matmul(xb, w32, 2048, 512, F32)
    p16 = _matmul(xb, w16, 2048, 1024, BF16)
    ya_in = _hgrn(p32, lb_logits, hg_norm_g[l], batch, seq)
    ckvn, ikn, qlat = _dsa_prep(p32, p16, kv_norm_g[l], ik_norm_g[l], ik_norm_b[l],
                                w_uk[l].astype(BF16), 1024)
    tq = 256
    iw_t = p32[:, C_IW:C_IW + IDX_HEADS].T
    ckvn_t = jnp.swapaxes(ckvn.reshape(batch, seq, KV_RANK), 1, 2)
    w_uv_t = jnp.swapaxes(w_uv[l], 1, 2).astype(BF16)
    near = _near_bias_table(rel_bias, tq)
    yb_in = _dsa(p16, iw_t, ikn, qlat, ckvn, ckvn_t, w_uv_t, near, batch, seq, tq)
    mix_in = _merge(ya_in, yb_in, w_branch_a[l].astype(BF16), w_branch_b[l].astype(BF16), p32, 1024, 512)
    h1 = _matmul_res_ln(mix_in, w_mix_out[l].astype(BF16), xf, ln1_g[l], ln1_b[l], 512)

    qc = _matmul(h1, w_cq[l].astype(BF16), 1024, 1024, BF16)
    memf = mem.reshape(batch * mlen, d).astype(BF16)
    kc = _matmul(memf, w_ck[l].astype(BF16), 1024, 1024, BF16)
    vc = _matmul(memf, w_cv[l].astype(BF16), 1024, 1024, BF16)
    ca_in = _xattn(qc, kc, vc, batch, seq, mlen, 512)
    h2 = _matmul_res_ln(ca_in, w_co[l].astype(BF16), h1, ln2_g[l], ln2_b[l], 512)

    qp = _matmul(h2, w_pq[l].astype(BF16), 1024, 1024, BF16)
    keys = peer_keys[l].reshape(2 * PEER_HEADS, PEER_NKEYS, PEER_DKEY // 2).astype(BF16)
    u = peer_u[l].astype(BF16)
    v_t = peer_v[l].astype(BF16).T
    h3 = _peer(qp, keys, h2, u, v_t, ln3_g[l], ln3_b[l], 512, 512, 256)
    return h3.reshape(batch, seq, d)
```

```python
import functools
import math

import jax
import jax.numpy as jnp
from jax import lax
from jax.experimental import pallas as pl
from jax.experimental.pallas import tpu as pltpu

F32 = jnp.float32
BF16 = jnp.bfloat16

D_MODEL = 2048
ALPHA = 2.0 ** 0.25
LN_EPS = 1e-5
RMS_EPS = 1e-6
HG_DK = 128
HG_HEADS = D_MODEL // 128
HG_WIDTH = HG_HEADS * HG_DK
HG_CHUNK = 64
AT_DH = 128
AT_HEADS = D_MODEL // 128
AT_WIDTH = AT_HEADS * AT_DH
KV_RANK = D_MODEL // 8
IDX_HEADS = 16
IDX_DIM = 128
TOPK_MAX = 256
REL_BUCKETS = 32
REL_MAX_DIST = 128
MEM_HEADS = 4
MEM_DH = D_MODEL // MEM_HEADS
PEER_HEADS = 8
PEER_NKEYS = 128
PEER_N = PEER_NKEYS * PEER_NKEYS
PEER_DKEY = 256
PEER_TOPK = 16

LANES = 128
VMEM_LIMIT = 56 * 1024 * 1024

C_HQ, C_HF, C_HI, C_HG = 0, HG_WIDTH, 2 * HG_WIDTH, 3 * HG_WIDTH
C_CKV = 4 * HG_WIDTH
C_IK = C_CKV + KV_RANK
C_IW = C_IK + IDX_DIM
C_GA = C_IW + LANES
C_GB = C_GA + D_MODEL
N_P32 = C_GB + D_MODEL

NEG_INF = float("-inf")


def _cparams(sem):
    return pltpu.CompilerParams(dimension_semantics=sem, vmem_limit_bytes=VMEM_LIMIT)


def _dot(a, b):
    return jnp.dot(a, b, preferred_element_type=F32)


def _dot_nt(a, b):
    return lax.dot_general(a, b, (((1,), (1,)), ((), ())), preferred_element_type=F32)


def _dot_tn(a, b):
    return lax.dot_general(a, b, (((0,), (0,)), ((), ())), preferred_element_type=F32)


def _mm_kernel(a_ref, w_ref, o_ref):
    o_ref[...] = _dot(a_ref[...].astype(BF16), w_ref[...]).astype(o_ref.dtype)


def _matmul(a, w, tm, tn, out_dtype):
    m, k = a.shape
    n = w.shape[1]
    return pl.pallas_call(
        _mm_kernel,
        grid=(m // tm, n // tn),
        in_specs=[pl.BlockSpec((tm, k), lambda i, j: (i, 0)),
                  pl.BlockSpec((k, tn), lambda i, j: (0, j))],
        out_specs=pl.BlockSpec((tm, tn), lambda i, j: (i, j)),
        out_shape=jax.ShapeDtypeStruct((m, n), out_dtype),
        compiler_params=_cparams(("parallel", "parallel")),
    )(a, w)


def _layer_norm_rows(y, g, b):
    mu = jnp.mean(y, axis=-1, keepdims=True)
    yc = y - mu
    var = jnp.mean(yc * yc, axis=-1, keepdims=True)
    return yc * lax.rsqrt(var + LN_EPS) * g + b


def _mm_res_ln_kernel(a_ref, w_ref, res_ref, g_ref, b_ref, o_ref):
    y = ALPHA * res_ref[...] + _dot(a_ref[...].astype(BF16), w_ref[...])
    o_ref[...] = _layer_norm_rows(y, g_ref[...], b_ref[...])


def _matmul_res_ln(a, w, res, g, b, tm):
    m, k = a.shape
    n = w.shape[1]
    return pl.pallas_call(
        _mm_res_ln_kernel,
        grid=(m // tm,),
        in_specs=[pl.BlockSpec((tm, k), lambda i: (i, 0)),
                  pl.BlockSpec((k, n), lambda i: (0, 0)),
                  pl.BlockSpec((tm, n), lambda i: (i, 0)),
                  pl.BlockSpec((1, n), lambda i: (0, 0)),
                  pl.BlockSpec((1, n), lambda i: (0, 0))],
        out_specs=pl.BlockSpec((tm, n), lambda i: (i, 0)),
        out_shape=jax.ShapeDtypeStruct((m, n), F32),
        compiler_params=_cparams(("parallel",)),
    )(a, w, res, g.reshape(1, n), b.reshape(1, n))


def _merge_kernel(a1_ref, a2_ref, w1_ref, w2_ref, ga_ref, gb_ref, o_ref):
    ya = _dot(a1_ref[...], w1_ref[...])
    yb = _dot(a2_ref[...], w2_ref[...])
    o_ref[...] = (jax.nn.sigmoid(ga_ref[...]) * ya + jax.nn.sigmoid(gb_ref[...]) * yb).astype(o_ref.dtype)


def _merge(a1, a2, w1, w2, p32, tm, tn):
    m, k = a1.shape
    n = w1.shape[1]
    ga_blk, gb_blk = C_GA // tn, C_GB // tn
    return pl.pallas_call(
        _merge_kernel,
        grid=(m // tm, n // tn),
        in_specs=[pl.BlockSpec((tm, k), lambda i, j: (i, 0)),
                  pl.BlockSpec((tm, k), lambda i, j: (i, 0)),
                  pl.BlockSpec((k, tn), lambda i, j: (0, j)),
                  pl.BlockSpec((k, tn), lambda i, j: (0, j)),
                  pl.BlockSpec((tm, tn), lambda i, j: (i, ga_blk + j)),
                  pl.BlockSpec((tm, tn), lambda i, j: (i, gb_blk + j))],
        out_specs=pl.BlockSpec((tm, tn), lambda i, j: (i, j)),
        out_shape=jax.ShapeDtypeStruct((m, n), BF16),
        compiler_params=_cparams(("parallel", "parallel")),
    )(a1, a2, w1, w2, p32, p32)


def _hgrn_kernel(q_ref, f_ref, i_ref, g_ref, lb_ref, ng_ref, o_ref, st_ref, *, hb, rb):
    c = HG_CHUNK

    @pl.when(pl.program_id(2) == 0)
    def _():
        st_ref[...] = jnp.zeros_like(st_ref)

    l0 = lb_ref[0:1, :]
    l1 = lb_ref[1:2, :]
    mx = jnp.maximum(l0, l1)
    e0 = jnp.exp(l0 - mx)
    lb = e0 / (e0 + jnp.exp(l1 - mx))

    f = lb + (1.0 - lb) * jax.nn.sigmoid(f_ref[...])
    kk = 1.0 - f
    gcum = jnp.log(f)
    pos = lax.broadcasted_iota(jnp.int32, gcum.shape, 0) & (c - 1)
    for s in (1, 2, 4, 8, 16, 32):
        gcum = gcum + jnp.where(pos >= s, pltpu.roll(gcum, s, axis=0), 0.0)

    tril = (lax.broadcasted_iota(jnp.int32, (c, c), 0) >= lax.broadcasted_iota(jnp.int32, (c, c), 1))
    ng = ng_ref[...]
    for ci in range(rb // c):
        rows = slice(ci * c, (ci + 1) * c)
        for h in range(hb):
            cols = slice(h * HG_DK, (h + 1) * HG_DK)
            gc = gcum[rows, cols]
            g_ref_row = gc[c // 2 - 1:c // 2, :]
            g_last = gc[c - 1:c, :]
            qc = q_ref[rows, cols]
            kc = kk[rows, cols]
            vc = i_ref[rows, cols].astype(BF16)
            qa = (qc * jnp.exp(gc - g_ref_row)).astype(BF16)
            kb = (kc * jnp.exp(g_ref_row - gc)).astype(BF16)
            a = jnp.where(tril, _dot_nt(qa, kb), 0.0)
            o = _dot(a.astype(BF16), vc)
            st = st_ref[h]
            qd = (qc * jnp.exp(gc)).astype(BF16)
            o = o + _dot_nt(qd, st.astype(BF16))
            kd = (kc * jnp.exp(g_last - gc)).astype(BF16)
            st_ref[h] = st * jnp.exp(g_last) + _dot_tn(vc, kd)
            o = o * lax.rsqrt(jnp.mean(o * o, axis=-1, keepdims=True) + RMS_EPS) * ng[:, cols]
            gate = g_ref[rows, cols]
            o_ref[rows, cols] = (o * (gate * jax.nn.sigmoid(gate))).astype(o_ref.dtype)


def _hgrn(p32, lb_logits, norm_g, batch, seq, hb=4, rb=512):
    t = batch * seq
    w = hb * HG_DK
    nr = seq // rb
    hq, hf, hi, hg = (cc // w for cc in (C_HQ, C_HF, C_HI, C_HG))
    row = lambda b, h, r: b * nr + r
    in_spec = lambda off: pl.BlockSpec((rb, w), lambda b, h, r: (row(b, h, r), off + h))
    return pl.pallas_call(
        functools.partial(_hgrn_kernel, hb=hb, rb=rb),
        grid=(batch, HG_HEADS // hb, nr),
        in_specs=[in_spec(hq), in_spec(hf), in_spec(hi), in_spec(hg),
                  pl.BlockSpec((2, w), lambda b, h, r: (0, h)),
                  pl.BlockSpec((1, w), lambda b, h, r: (0, h))],
        out_specs=pl.BlockSpec((rb, w), lambda b, h, r: (row(b, h, r), h)),
        out_shape=jax.ShapeDtypeStruct((t, HG_WIDTH), BF16),
        scratch_shapes=[pltpu.VMEM((hb, HG_DK, HG_DK), F32)],
        compiler_params=_cparams(("parallel", "parallel", "arbitrary")),
    )(p32, p32, p32, p32, lb_logits, norm_g.reshape(1, HG_WIDTH))


def _dsa_prep_kernel(ckv_ref, ik_ref, aq_ref, kvg_ref, ikg_ref, ikb_ref, wuk_ref,
                     ckvn_ref, ikn_ref, qlat_ref):
    ckv = ckv_ref[...]
    ckvn_ref[...] = (ckv * lax.rsqrt(jnp.mean(ckv * ckv, axis=-1, keepdims=True) + RMS_EPS)
                     * kvg_ref[...]).astype(ckvn_ref.dtype)
    ikn_ref[...] = _layer_norm_rows(ik_ref[...], ikg_ref[...], ikb_ref[...]).astype(ikn_ref.dtype)
    scale = AT_DH ** -0.5
    for h in range(AT_HEADS):
        qh = aq_ref[:, h * AT_DH:(h + 1) * AT_DH]
        qlat_ref[:, h * KV_RANK:(h + 1) * KV_RANK] = (_dot_nt(qh, wuk_ref[h]) * scale).astype(qlat_ref.dtype)


def _dsa_prep(p32, p16, kv_g, ik_g, ik_b, w_uk, tm):
    t = p32.shape[0]
    return pl.pallas_call(
        _dsa_prep_kernel,
        grid=(t // tm,),
        in_specs=[pl.BlockSpec((tm, KV_RANK), lambda i: (i, C_CKV // KV_RANK)),
                  pl.BlockSpec((tm, IDX_DIM), lambda i: (i, C_IK // IDX_DIM)),
                  pl.BlockSpec((tm, AT_WIDTH), lambda i: (i, 0)),
                  pl.BlockSpec((1, KV_RANK), lambda i: (0, 0)),
                  pl.BlockSpec((1, IDX_DIM), lambda i: (0, 0)),
                  pl.BlockSpec((1, IDX_DIM), lambda i: (0, 0)),
                  pl.BlockSpec((AT_HEADS, KV_RANK, AT_DH), lambda i: (0, 0, 0))],
        out_specs=[pl.BlockSpec((tm, KV_RANK), lambda i: (i, 0)),
                   pl.BlockSpec((tm, IDX_DIM), lambda i: (i, 0)),
                   pl.BlockSpec((tm, AT_HEADS * KV_RANK), lambda i: (i, 0))],
        out_shape=[jax.ShapeDtypeStruct((t, KV_RANK), BF16),
                   jax.ShapeDtypeStruct((t, IDX_DIM), BF16),
                   jax.ShapeDtypeStruct((t, AT_HEADS * KV_RANK), BF16)],
        compiler_params=_cparams(("parallel",)),
    )(p32, p32, p16, kv_g.reshape(1, -1), ik_g.reshape(1, -1), ik_b.reshape(1, -1), w_uk)


def _dsa_kernel(iq_ref, iwt_ref, ik_ref, qlat_ref, ckv_ref, ckvt_ref, wuvt_ref, near_ref,
                o_ref, key_ref, lg_ref, *, tq, seq, topk, tile_lo):
    i = tile_lo + pl.program_id(1)
    t0 = i * tq
    ik = ik_ref[...]
    w_scale = (IDX_DIM ** -0.5) * (IDX_HEADS ** -0.5)
    sc = jnp.zeros((seq, tq), F32)
    for h in range(IDX_HEADS):
        s = _dot_nt(ik, iq_ref[:, h * IDX_DIM:(h + 1) * IDX_DIM])
        sc = sc + jnp.maximum(s, 0.0) * (iwt_ref[h:h + 1, :] * w_scale)
    kpos = lax.broadcasted_iota(jnp.int32, (seq, tq), 0)
    tpos = t0 + lax.broadcasted_iota(jnp.int32, (seq, tq), 1)
    causal = kpos <= tpos
    sc = jnp.where(causal, sc + 0.0, NEG_INF)
    bits = pltpu.bitcast(sc, jnp.int32)
    key_ref[...] = bits ^ ((bits >> 31) & jnp.int32(0x7FFFFFFF))

    def bit_step(it, thr):
        cand = thr + jnp.left_shift(jnp.int32(1), 31 - it)
        cnt = jnp.sum((key_ref[...] >= cand).astype(F32), axis=0, keepdims=True)
        return jnp.where(cnt >= float(topk), cand, thr)

    thr = lax.fori_loop(0, 32, bit_step, jnp.full((1, tq), jnp.iinfo(jnp.int32).min, jnp.int32))
    addmask = jnp.where((key_ref[...] >= thr) & causal, 0.0, NEG_INF)

    wn = tq + REL_MAX_DIST
    for h in range(AT_HEADS):
        lg_h = lg_ref.at[h % 2]
        lg_h[...] = _dot_nt(ckv_ref[...], qlat_ref[:, h * KV_RANK:(h + 1) * KV_RANK]) + addmask
        if tile_lo == 0:
            lg_h[0:tq, :] += near_ref[h, REL_MAX_DIST:wn, :]
        else:
            start = pl.multiple_of(t0 - REL_MAX_DIST, REL_MAX_DIST)
            lg_h[pl.ds(start, wn), :] += near_ref[h]
        lg = lg_h[...]
        p = jnp.exp(lg - jnp.max(lg, axis=0, keepdims=True))
        denom = jnp.sum(p, axis=0, keepdims=True)
        olat = _dot(ckvt_ref[...], p.astype(BF16)) / denom
        out_t = _dot(wuvt_ref[h], olat.astype(BF16))
        o_ref[:, h * AT_DH:(h + 1) * AT_DH] = out_t.T.astype(o_ref.dtype)


def _near_bias_table(rel_bias, tq):
    wn = tq + REL_MAX_DIST
    d = (jnp.arange(tq, dtype=jnp.int32)[None, :] - jnp.arange(wn, dtype=jnp.int32)[:, None]
         + REL_MAX_DIST)
    max_exact = REL_BUCKETS // 2
    dd = jnp.maximum(d, 0)
    large = max_exact + (jnp.log(jnp.maximum(dd, 1).astype(F32) / max_exact)
                         / math.log(REL_MAX_DIST / max_exact) * (REL_BUCKETS - max_exact)).astype(jnp.int32)
    large = jnp.minimum(large, REL_BUCKETS - 1)
    bucket = jnp.where(dd < max_exact, dd, large)
    rb = rel_bias.astype(F32) - rel_bias.astype(F32)[REL_BUCKETS - 1]
    onehot = (bucket[None, :, :] == jnp.arange(REL_BUCKETS, dtype=jnp.int32)[:, None, None]).astype(F32)
    return jnp.einsum("bh,bcr->hcr", rb, onehot, precision=lax.Precision.HIGHEST)


def _dsa_group(p16, iw_t, ikn, qlat, ckvn, ckvn_t, w_uv_t, near, batch, seq, tq, tile_lo, n_tiles):
    nq = seq // tq
    klen = (tile_lo + n_tiles) * tq
    topk = min(TOPK_MAX, seq // 4)
    wn = tq + REL_MAX_DIST
    qrow = lambda b, i: b * nq + tile_lo + i
    return pl.pallas_call(
        functools.partial(_dsa_kernel, tq=tq, seq=klen, topk=topk, tile_lo=tile_lo),
        grid=(batch, n_tiles),
        in_specs=[pl.BlockSpec((tq, IDX_HEADS * IDX_DIM), lambda b, i: (qrow(b, i), 1)),
                  pl.BlockSpec((IDX_HEADS, tq), lambda b, i: (0, qrow(b, i))),
                  pl.BlockSpec((None, klen, IDX_DIM), lambda b, i: (b, 0, 0)),
                  pl.BlockSpec((tq, AT_HEADS * KV_RANK), lambda b, i: (qrow(b, i), 0)),
                  pl.BlockSpec((None, klen, KV_RANK), lambda b, i: (b, 0, 0)),
                  pl.BlockSpec((None, KV_RANK, klen), lambda b, i: (b, 0, 0)),
                  pl.BlockSpec((AT_HEADS, AT_DH, KV_RANK), lambda b, i: (0, 0, 0)),
                  pl.BlockSpec((AT_HEADS, wn, tq), lambda b, i: (0, 0, 0))],
        out_specs=pl.BlockSpec((None, tq, AT_WIDTH), lambda b, i: (b, i, 0)),
        out_shape=jax.ShapeDtypeStruct((batch, n_tiles * tq, AT_WIDTH), BF16),
        scratch_shapes=[pltpu.VMEM((klen, tq), jnp.int32), pltpu.VMEM((2, klen, tq), F32)],
        compiler_params=_cparams(("parallel", "arbitrary")),
    )(p16, iw_t, ikn.reshape(batch, seq, IDX_DIM), qlat, ckvn.reshape(batch, seq, KV_RANK),
      ckvn_t, w_uv_t, near)


def _dsa(p16, iw_t, ikn, qlat, ckvn, ckvn_t, w_uv_t, near, batch, seq, tq):
    nq = seq // tq
    bounds = [0, 1] + list(range(2, nq + 1, 2))
    if bounds[-1] != nq:
        bounds.append(nq)
    outs = [_dsa_group(p16, iw_t, ikn, qlat, ckvn, ckvn_t, w_uv_t, near, batch, seq, tq, lo, hi - lo)
            for lo, hi in zip(bounds[:-1], bounds[1:])]
    return jnp.concatenate(outs, axis=1).reshape(batch * seq, AT_WIDTH)


def _xattn_kernel(q_ref, k_ref, v_ref, o_ref):
    scale = MEM_DH ** -0.5
    for h in range(MEM_HEADS):
        cols = slice(h * MEM_DH, (h + 1) * MEM_DH)
        lg = _dot_nt(q_ref[:, cols], k_ref[:, cols]) * scale
        p = jnp.exp(lg - jnp.max(lg, axis=-1, keepdims=True))
        p = p / jnp.sum(p, axis=-1, keepdims=True)
        o_ref[:, cols] = _dot(p.astype(BF16), v_ref[:, cols]).astype(o_ref.dtype)


def _xattn(q, k, v, batch, seq, mlen, tq):
    t = batch * seq
    nq = seq // tq
    return pl.pallas_call(
        _xattn_kernel,
        grid=(batch, nq),
        in_specs=[pl.BlockSpec((tq, D_MODEL), lambda b, i: (b * nq + i, 0)),
                  pl.BlockSpec((mlen, D_MODEL), lambda b, i: (b, 0)),
                  pl.BlockSpec((mlen, D_MODEL), lambda b, i: (b, 0))],
        out_specs=pl.BlockSpec((tq, D_MODEL), lambda b, i: (b * nq + i, 0)),
        out_shape=jax.ShapeDtypeStruct((t, D_MODEL), BF16),
        compiler_params=_cparams(("parallel", "parallel")),
    )(q, k, v)


def _top_values(x, k):
    rows = lax.broadcasted_iota(jnp.int32, (k, x.shape[1]), 0)
    out = jnp.zeros((k, x.shape[1]), F32)
    for a in range(k):
        mx = jnp.max(x, axis=0, keepdims=True)
        out = jnp.where(rows == a, mx, out)
        x = jnp.where(x == mx, NEG_INF, x)
    return out


def _peer_kernel(qp_ref, keys_ref, xt_ref, u_ref, vt_ref, res_ref, g_ref, b_ref, o_ref,
                 s_ref, e_ref, th_ref, acc_ref, wg_ref, *, tt, te, tc):
    e = pl.program_id(1)
    nk = PEER_NKEYS
    k = PEER_TOPK

    @pl.when(e == 0)
    def _():
        acc_ref[...] = jnp.zeros_like(acc_ref)
        for hp in range(2 * PEER_HEADS):
            s_ref[hp] = _dot_nt(keys_ref[hp], qp_ref[:, hp * nk:(hp + 1) * nk])
        for h in range(PEER_HEADS):
            s0 = s_ref[2 * h]
            s1 = s_ref[2 * h + 1]
            v0 = _top_values(s0, k)
            v1 = _top_values(s1, k)
            half = k // 2
            cand = jnp.concatenate(
                [v0[0:1, :] + v1]
                + [v0[a:a + 1, :] + v1[0:half, :] for a in range(1, half)]
                + [v0[half:k, :] + v1[0:1, :]], axis=0)
            top = _top_values(cand, k)
            z = jnp.sum(jnp.exp(top - top[0:1, :]), axis=0, keepdims=True)
            th_ref[h:h + 1, :] = top[k - 1:k, :]
            e_ref[2 * h] = jnp.exp(s0 - v0[0:1, :]) / z
            e_ref[2 * h + 1] = jnp.exp(s1 - v1[0:1, :])

    hts = [_dot(u_ref[c * tc:(c + 1) * tc, :], xt_ref[...]) for c in range(te // tc)]
    acc = acc_ref[...]
    for c in range(te // tc):
        rows = slice(c * tc, (c + 1) * tc)
        ht = hts[c]
        gel = 0.5 * ht * (1.0 + lax.erf(ht * (2.0 ** -0.5)))
        for ii in range(tc // nk):
            irow = e * (te // nk) + c * (tc // nk) + ii
            sub = slice(ii * nk, (ii + 1) * nk)
            s0rows = [s_ref[2 * h, pl.ds(irow, 1), :] for h in range(PEER_HEADS)]
            e0rows = [e_ref[2 * h, pl.ds(irow, 1), :] for h in range(PEER_HEADS)]
            for lc in range(tt // LANES):
                tok = slice(lc * LANES, (lc + 1) * LANES)
                w = jnp.zeros((nk, LANES), F32)
                for h in range(PEER_HEADS):
                    ssum = s0rows[h][:, tok] + s_ref[2 * h + 1, :, tok]
                    w = w + jnp.where(ssum >= th_ref[h:h + 1, tok],
                                      e0rows[h][:, tok] * e_ref[2 * h + 1, :, tok], 0.0)
                wg_ref[c * tc + ii * nk:c * tc + (ii + 1) * nk, tok] = (w * gel[sub, tok]).astype(BF16)
        acc = acc + _dot(vt_ref[:, rows], wg_ref[rows, :])
    acc_ref[...] = acc

    @pl.when(e == pl.num_programs(1) - 1)
    def _():
        for c in range(tt // tc):
            tok = slice(c * tc, (c + 1) * tc)
            y = ALPHA * res_ref[tok, :] + acc_ref[:, tok].T
            o_ref[tok, :] = _layer_norm_rows(y, g_ref[...], b_ref[...])


def _peer(qp, keys, h2, u, v_t, g, b, tt, te, tc):
    t, d = h2.shape
    n = u.shape[0]
    return pl.pallas_call(
        functools.partial(_peer_kernel, tt=tt, te=te, tc=tc),
        grid=(t // tt, n // te),
        in_specs=[pl.BlockSpec((tt, PEER_HEADS * PEER_DKEY), lambda i, e: (i, 0), pipeline_mode=pl.Buffered(1)),
                  pl.BlockSpec((2 * PEER_HEADS, PEER_NKEYS, PEER_DKEY // 2), lambda i, e: (0, 0, 0)),
                  pl.BlockSpec((d, tt), lambda i, e: (0, i), pipeline_mode=pl.Buffered(1)),
                  pl.BlockSpec((te, d), lambda i, e: (e, 0)),
                  pl.BlockSpec((d, te), lambda i, e: (0, e)),
                  pl.BlockSpec((tt, d), lambda i, e: (i, 0), pipeline_mode=pl.Buffered(1)),
                  pl.BlockSpec((1, d), lambda i, e: (0, 0)),
                  pl.BlockSpec((1, d), lambda i, e: (0, 0))],
        out_specs=pl.BlockSpec((tt, d), lambda i, e: (i, 0)),
        out_shape=jax.ShapeDtypeStruct((t, d), F32),
        scratch_shapes=[pltpu.VMEM((2 * PEER_HEADS, PEER_NKEYS, tt), F32),
                        pltpu.VMEM((2 * PEER_HEADS, PEER_NKEYS, tt), F32),
                        pltpu.VMEM((PEER_HEADS, tt), F32),
                        pltpu.VMEM((d, tt), F32),
                        pltpu.VMEM((te, tt), BF16)],
        compiler_params=_cparams(("parallel", "arbitrary")),
    )(qp, keys, h2.astype(BF16).T, u, v_t, h2, g.reshape(1, d), b.reshape(1, d))


def kernel(x, mem, w_in, lb_logits, hg_norm_g, kv_norm_g, ik_norm_g, ik_norm_b, w_uk, w_uv,
           rel_bias, w_branch_a, w_branch_b, w_mix_out, ln1_g, ln1_b, w_cq, w_ck, w_cv, w_co,
           ln2_g, ln2_b, w_pq, peer_keys, peer_u, peer_v, ln3_g, ln3_b):
    batch, seq, d = x.shape
    mlen = mem.shape[1]
    t = batch * seq
    l = 0

    wi = w_in[l]
    o_aq = 4 * HG_WIDTH
    o_ckv = o_aq + AT_WIDTH
    o_iq = o_ckv + KV_RANK
    o_ik = o_iq + IDX_HEADS * IDX_DIM
    o_iw = o_ik + IDX_DIM
    o_ga = o_iw + IDX_HEADS
    w32 = jnp.concatenate([wi[:, :o_aq], wi[:, o_ckv:o_iq], wi[:, o_ik:o_iw],
                           jnp.pad(wi[:, o_iw:o_ga], ((0, 0), (0, LANES - IDX_HEADS))),
                           wi[:, o_ga:]], axis=1).astype(BF16)
    w16 = jnp.concatenate([wi[:, o_aq:o_ckv], wi[:, o_iq:o_ik]], axis=1).astype(BF16)
    xf = x.reshape(t, d)
    xb = xf.astype(BF16)

    p32 = _matmul(xb, w32, 2048, 512, F32)
    p16 = _matmul(xb, w16, 2048, 1024, BF16)
    ya_in = _hgrn(p32, lb_logits, hg_norm_g[l], batch, seq)
    ckvn, ikn, qlat = _dsa_prep(p32, p16, kv_norm_g[l], ik_norm_g[l], ik_norm_b[l],
                                w_uk[l].astype(BF16), 1024)
    tq = 256
    iw_t = p32[:, C_IW:C_IW + IDX_HEADS].T
    ckvn_t = jnp.swapaxes(ckvn.reshape(batch, seq, KV_RANK), 1, 2)
    w_uv_t = jnp.swapaxes(w_uv[l], 1, 2).astype(BF16)
    near = _near_bias_table(rel_bias, tq)
    yb_in = _dsa(p16, iw_t, ikn, qlat, ckvn, ckvn_t, w_uv_t, near, batch, seq, tq)
    mix_in = _merge(ya_in, yb_in, w_branch_a[l].astype(BF16), w_branch_b[l].astype(BF16), p32, 1024, 512)
    h1 = _matmul_res_ln(mix_in, w_mix_out[l].astype(BF16), xf, ln1_g[l], ln1_b[l], 512)

    qc = _matmul(h1, w_cq[l].astype(BF16), 1024, 1024, BF16)
    memf = mem.reshape(batch * mlen, d).astype(BF16)
    kc = _matmul(memf, w_ck[l].astype(BF16), 1024, 1024, BF16)
    vc = _matmul(memf, w_cv[l].astype(BF16), 1024, 1024, BF16)
    ca_in = _xattn(qc, kc, vc, batch, seq, mlen, 512)
    h2 = _matmul_res_ln(ca_in, w_co[l].astype(BF16), h1, ln2_g[l], ln2_b[l], 512)

    qp = _matmul(h2, w_pq[l].astype(BF16), 1024, 1024, BF16)
    keys = peer_keys[l].reshape(2 * PEER_HEADS, PEER_NKEYS, PEER_DKEY // 2).astype(BF16)
    u = peer_u[l].astype(BF16)
    v_t = peer_v[l].astype(BF16).T
    h3 = _peer(qp, keys, h2, u, v_t, ln3_g[l], ln3_b[l], 512, 1024, 256)
    return h3.reshape(batch, seq, d)
```

```python
import functools
import math

import jax
import jax.numpy as jnp
from jax import lax
from jax.experimental import pallas as pl
from jax.experimental.pallas import tpu as pltpu

F32 = jnp.float32
BF16 = jnp.bfloat16

D_MODEL = 2048
ALPHA = 2.0 ** 0.25
LN_EPS = 1e-5
RMS_EPS = 1e-6
HG_DK = 128
HG_HEADS = D_MODEL // 128
HG_WIDTH = HG_HEADS * HG_DK
HG_CHUNK = 64
AT_DH = 128
AT_HEADS = D_MODEL // 128
AT_WIDTH = AT_HEADS * AT_DH
KV_RANK = D_MODEL // 8
IDX_HEADS = 16
IDX_DIM = 128
TOPK_MAX = 256
REL_BUCKETS = 32
REL_MAX_DIST = 128
MEM_HEADS = 4
MEM_DH = D_MODEL // MEM_HEADS
PEER_HEADS = 8
PEER_NKEYS = 128
PEER_N = PEER_NKEYS * PEER_NKEYS
PEER_DKEY = 256
PEER_TOPK = 16

LANES = 128
VMEM_LIMIT = 56 * 1024 * 1024

C_HQ, C_HF, C_HI, C_HG = 0, HG_WIDTH, 2 * HG_WIDTH, 3 * HG_WIDTH
C_CKV = 4 * HG_WIDTH
C_IK = C_CKV + KV_RANK
C_IW = C_IK + IDX_DIM
C_GA = C_IW + LANES
C_GB = C_GA + D_MODEL
N_P32 = C_GB + D_MODEL

NEG_INF = float("-inf")


def _cparams(sem):
    return pltpu.CompilerParams(dimension_semantics=sem, vmem_limit_bytes=VMEM_LIMIT)


def _dot(a, b):
    return jnp.dot(a, b, preferred_element_type=F32)


def _dot_nt(a, b):
    return lax.dot_general(a, b, (((1,), (1,)), ((), ())), preferred_element_type=F32)


def _dot_tn(a, b):
    return lax.dot_general(a, b, (((0,), (0,)), ((), ())), preferred_element_type=F32)


def _mm_kernel(a_ref, w_ref, o_ref):
    o_ref[...] = _dot(a_ref[...].astype(BF16), w_ref[...]).astype(o_ref.dtype)


def _matmul(a, w, tm, tn, out_dtype):
    m, k = a.shape
    n = w.shape[1]
    return pl.pallas_call(
        _mm_kernel,
        grid=(m // tm, n // tn),
        in_specs=[pl.BlockSpec((tm, k), lambda i, j: (i, 0)),
                  pl.BlockSpec((k, tn), lambda i, j: (0, j))],
        out_specs=pl.BlockSpec((tm, tn), lambda i, j: (i, j)),
        out_shape=jax.ShapeDtypeStruct((m, n), out_dtype),
        compiler_params=_cparams(("parallel", "parallel")),
    )(a, w)


def _layer_norm_rows(y, g, b):
    mu = jnp.mean(y, axis=-1, keepdims=True)
    yc = y - mu
    var = jnp.mean(yc * yc, axis=-1, keepdims=True)
    return yc * lax.rsqrt(var + LN_EPS) * g + b


def _mm_res_ln_kernel(a_ref, w_ref, res_ref, g_ref, b_ref, o_ref):
    y = ALPHA * res_ref[...] + _dot(a_ref[...].astype(BF16), w_ref[...])
    o_ref[...] = _layer_norm_rows(y, g_ref[...], b_ref[...])


def _matmul_res_ln(a, w, res, g, b, tm):
    m, k = a.shape
    n = w.shape[1]
    return pl.pallas_call(
        _mm_res_ln_kernel,
        grid=(m // tm,),
        in_specs=[pl.BlockSpec((tm, k), lambda i: (i, 0)),
                  pl.BlockSpec((k, n), lambda i: (0, 0)),
                  pl.BlockSpec((tm, n), lambda i: (i, 0)),
                  pl.BlockSpec((1, n), lambda i: (0, 0)),
                  pl.BlockSpec((1, n), lambda i: (0, 0))],
        out_specs=pl.BlockSpec((tm, n), lambda i: (i, 0)),
        out_shape=jax.ShapeDtypeStruct((m, n), F32),
        compiler_params=_cparams(("parallel",)),
    )(a, w, res, g.reshape(1, n), b.reshape(1, n))


def _merge_kernel(a1_ref, a2_ref, w1_ref, w2_ref, ga_ref, gb_ref, o_ref):
    ya = _dot(a1_ref[...], w1_ref[...])
    yb = _dot(a2_ref[...], w2_ref[...])
    o_ref[...] = (jax.nn.sigmoid(ga_ref[...]) * ya + jax.nn.sigmoid(gb_ref[...]) * yb).astype(o_ref.dtype)


def _merge(a1, a2, w1, w2, p32, tm, tn):
    m, k = a1.shape
    n = w1.shape[1]
    ga_blk, gb_blk = C_GA // tn, C_GB // tn
    return pl.pallas_call(
        _merge_kernel,
        grid=(m // tm, n // tn),
        in_specs=[pl.BlockSpec((tm, k), lambda i, j: (i, 0)),
                  pl.BlockSpec((tm, k), lambda i, j: (i, 0)),
                  pl.BlockSpec((k, tn), lambda i, j: (0, j)),
                  pl.BlockSpec((k, tn), lambda i, j: (0, j)),
                  pl.BlockSpec((tm, tn), lambda i, j: (i, ga_blk + j)),
                  pl.BlockSpec((tm, tn), lambda i, j: (i, gb_blk + j))],
        out_specs=pl.BlockSpec((tm, tn), lambda i, j: (i, j)),
        out_shape=jax.ShapeDtypeStruct((m, n), BF16),
        compiler_params=_cparams(("parallel", "parallel")),
    )(a1, a2, w1, w2, p32, p32)


def _hgrn_kernel(q_ref, f_ref, i_ref, g_ref, lb_ref, ng_ref, o_ref, st_ref, *, hb, rb):
    c = HG_CHUNK

    @pl.when(pl.program_id(2) == 0)
    def _():
        st_ref[...] = jnp.zeros_like(st_ref)

    l0 = lb_ref[0:1, :]
    l1 = lb_ref[1:2, :]
    mx = jnp.maximum(l0, l1)
    e0 = jnp.exp(l0 - mx)
    lb = e0 / (e0 + jnp.exp(l1 - mx))

    f = lb + (1.0 - lb) * jax.nn.sigmoid(f_ref[...])
    kk = 1.0 - f
    gcum = jnp.log(f)
    pos = lax.broadcasted_iota(jnp.int32, gcum.shape, 0) & (c - 1)
    for s in (1, 2, 4, 8, 16, 32):
        gcum = gcum + jnp.where(pos >= s, pltpu.roll(gcum, s, axis=0), 0.0)

    tril = (lax.broadcasted_iota(jnp.int32, (c, c), 0) >= lax.broadcasted_iota(jnp.int32, (c, c), 1))
    ng = ng_ref[...]
    for ci in range(rb // c):
        rows = slice(ci * c, (ci + 1) * c)
        for h in range(hb):
            cols = slice(h * HG_DK, (h + 1) * HG_DK)
            gc = gcum[rows, cols]
            g_ref_row = gc[c // 2 - 1:c // 2, :]
            g_last = gc[c - 1:c, :]
            qc = q_ref[rows, cols]
            kc = kk[rows, cols]
            vc = i_ref[rows, cols].astype(BF16)
            qa = (qc * jnp.exp(gc - g_ref_row)).astype(BF16)
            kb = (kc * jnp.exp(g_ref_row - gc)).astype(BF16)
            a = jnp.where(tril, _dot_nt(qa, kb), 0.0)
            o = _dot(a.astype(BF16), vc)
            st = st_ref[h]
            qd = (qc * jnp.exp(gc)).astype(BF16)
            o = o + _dot_nt(qd, st.astype(BF16))
            kd = (kc * jnp.exp(g_last - gc)).astype(BF16)
            st_ref[h] = st * jnp.exp(g_last) + _dot_tn(vc, kd)
            o = o * lax.rsqrt(jnp.mean(o * o, axis=-1, keepdims=True) + RMS_EPS) * ng[:, cols]
            gate = g_ref[rows, cols]
            o_ref[rows, cols] = (o * (gate * jax.nn.sigmoid(gate))).astype(o_ref.dtype)


def _hgrn(p32, lb_logits, norm_g, batch, seq, hb=4, rb=512):
    t = batch * seq
    w = hb * HG_DK
    nr = seq // rb
    hq, hf, hi, hg = (cc // w for cc in (C_HQ, C_HF, C_HI, C_HG))
    row = lambda b, h, r: b * nr + r
    in_spec = lambda off: pl.BlockSpec((rb, w), lambda b, h, r: (row(b, h, r), off + h))
    return pl.pallas_call(
        functools.partial(_hgrn_kernel, hb=hb, rb=rb),
        grid=(batch, HG_HEADS // hb, nr),
        in_specs=[in_spec(hq), in_spec(hf), in_spec(hi), in_spec(hg),
                  pl.BlockSpec((2, w), lambda b, h, r: (0, h)),
                  pl.BlockSpec((1, w), lambda b, h, r: (0, h))],
        out_specs=pl.BlockSpec((rb, w), lambda b, h, r: (row(b, h, r), h)),
        out_shape=jax.ShapeDtypeStruct((t, HG_WIDTH), BF16),
        scratch_shapes=[pltpu.VMEM((hb, HG_DK, HG_DK), F32)],
        compiler_params=_cparams(("parallel", "parallel", "arbitrary")),
    )(p32, p32, p32, p32, lb_logits, norm_g.reshape(1, HG_WIDTH))


def _dsa_prep_kernel(ckv_ref, ik_ref, aq_ref, kvg_ref, ikg_ref, ikb_ref, wuk_ref,
                     ckvn_ref, ikn_ref, qlat_ref):
    ckv = ckv_ref[...]
    ckvn_ref[...] = (ckv * lax.rsqrt(jnp.mean(ckv * ckv, axis=-1, keepdims=True) + RMS_EPS)
                     * kvg_ref[...]).astype(ckvn_ref.dtype)
    ikn_ref[...] = _layer_norm_rows(ik_ref[...], ikg_ref[...], ikb_ref[...]).astype(ikn_ref.dtype)
    scale = AT_DH ** -0.5
    for h in range(AT_HEADS):
        qh = aq_ref[:, h * AT_DH:(h + 1) * AT_DH]
        qlat_ref[:, h * KV_RANK:(h + 1) * KV_RANK] = (_dot_nt(qh, wuk_ref[h]) * scale).astype(qlat_ref.dtype)


def _dsa_prep(p32, p16, kv_g, ik_g, ik_b, w_uk, tm):
    t = p32.shape[0]
    return pl.pallas_call(
        _dsa_prep_kernel,
        grid=(t // tm,),
        in_specs=[pl.BlockSpec((tm, KV_RANK), lambda i: (i, C_CKV // KV_RANK)),
                  pl.BlockSpec((tm, IDX_DIM), lambda i: (i, C_IK // IDX_DIM)),
                  pl.BlockSpec((tm, AT_WIDTH), lambda i: (i, 0)),
                  pl.BlockSpec((1, KV_RANK), lambda i: (0, 0)),
                  pl.BlockSpec((1, IDX_DIM), lambda i: (0, 0)),
                  pl.BlockSpec((1, IDX_DIM), lambda i: (0, 0)),
                  pl.BlockSpec((AT_HEADS, KV_RANK, AT_DH), lambda i: (0, 0, 0))],
        out_specs=[pl.BlockSpec((tm, KV_RANK), lambda i: (i, 0)),
                   pl.BlockSpec((tm, IDX_DIM), lambda i: (i, 0)),
                   pl.BlockSpec((tm, AT_HEADS * KV_RANK), lambda i: (i, 0))],
        out_shape=[jax.ShapeDtypeStruct((t, KV_RANK), BF16),
                   jax.ShapeDtypeStruct((t, IDX_DIM), BF16),
                   jax.ShapeDtypeStruct((t, AT_HEADS * KV_RANK), BF16)],
        compiler_params=_cparams(("parallel",)),
    )(p32, p32, p16, kv_g.reshape(1, -1), ik_g.reshape(1, -1), ik_b.reshape(1, -1), w_uk)


def _dsa_kernel(iq_ref, iwt_ref, ik_ref, qlat_ref, ckv_ref, ckvt_ref, wuvt_ref, near_ref,
                o_ref, key_ref, lg_ref, *, tq, seq, topk, tile_lo):
    i = tile_lo + pl.program_id(1)
    t0 = i * tq
    ik = ik_ref[...]
    w_scale = (IDX_DIM ** -0.5) * (IDX_HEADS ** -0.5)
    sc = jnp.zeros((seq, tq), F32)
    for h in range(IDX_HEADS):
        s = _dot_nt(ik, iq_ref[:, h * IDX_DIM:(h + 1) * IDX_DIM])
        sc = sc + jnp.maximum(s, 0.0) * (iwt_ref[h:h + 1, :] * w_scale)
    kpos = lax.broadcasted_iota(jnp.int32, (seq, tq), 0)
    tpos = t0 + lax.broadcasted_iota(jnp.int32, (seq, tq), 1)
    causal = kpos <= tpos
    sc = jnp.where(causal, sc + 0.0, NEG_INF)
    bits = pltpu.bitcast(sc, jnp.int32)
    key_ref[...] = bits ^ ((bits >> 31) & jnp.int32(0x7FFFFFFF))

    def bit_step(it, thr):
        cand = thr + jnp.left_shift(jnp.int32(1), 31 - it)
        cnt = jnp.sum((key_ref[...] >= cand).astype(F32), axis=0, keepdims=True)
        return jnp.where(cnt >= float(topk), cand, thr)

    thr = lax.fori_loop(0, 32, bit_step, jnp.full((1, tq), jnp.iinfo(jnp.int32).min, jnp.int32))
    addmask = jnp.where((key_ref[...] >= thr) & causal, 0.0, NEG_INF)

    wn = tq + REL_MAX_DIST
    for h in range(AT_HEADS):
        lg_h = lg_ref.at[h % 2]
        lg_h[...] = _dot_nt(ckv_ref[...], qlat_ref[:, h * KV_RANK:(h + 1) * KV_RANK]) + addmask
        if tile_lo == 0:
            lg_h[0:tq, :] += near_ref[h, REL_MAX_DIST:wn, :]
        else:
            start = pl.multiple_of(t0 - REL_MAX_DIST, REL_MAX_DIST)
            lg_h[pl.ds(start, wn), :] += near_ref[h]
        lg = lg_h[...]
        p = jnp.exp(lg - jnp.max(lg, axis=0, keepdims=True))
        denom = jnp.sum(p, axis=0, keepdims=True)
        olat = _dot(ckvt_ref[...], p.astype(BF16)) / denom
        out_t = _dot(wuvt_ref[h], olat.astype(BF16))
        o_ref[:, h * AT_DH:(h + 1) * AT_DH] = out_t.T.astype(o_ref.dtype)


def _near_bias_table(rel_bias, tq):
    wn = tq + REL_MAX_DIST
    d = (jnp.arange(tq, dtype=jnp.int32)[None, :] - jnp.arange(wn, dtype=jnp.int32)[:, None]
         + REL_MAX_DIST)
    max_exact = REL_BUCKETS // 2
    dd = jnp.maximum(d, 0)
    large = max_exact + (jnp.log(jnp.maximum(dd, 1).astype(F32) / max_exact)
                         / math.log(REL_MAX_DIST / max_exact) * (REL_BUCKETS - max_exact)).astype(jnp.int32)
    large = jnp.minimum(large, REL_BUCKETS - 1)
    bucket = jnp.where(dd < max_exact, dd, large)
    rb = rel_bias.astype(F32) - rel_bias.astype(F32)[REL_BUCKETS - 1]
    onehot = (bucket[None, :, :] == jnp.arange(REL_BUCKETS, dtype=jnp.int32)[:, None, None]).astype(F32)
    return jnp.einsum("bh,bcr->hcr", rb, onehot, precision=lax.Precision.HIGHEST)


def _dsa_group(p16, iw_t, ikn, qlat, ckvn, ckvn_t, w_uv_t, near, batch, seq, tq, tile_lo, n_tiles):
    nq = seq // tq
    klen = (tile_lo + n_tiles) * tq
    topk = min(TOPK_MAX, seq // 4)
    wn = tq + REL_MAX_DIST
    qrow = lambda b, i: b * nq + tile_lo + i
    return pl.pallas_call(
        functools.partial(_dsa_kernel, tq=tq, seq=klen, topk=topk, tile_lo=tile_lo),
        grid=(batch, n_tiles),
        in_specs=[pl.BlockSpec((tq, IDX_HEADS * IDX_DIM), lambda b, i: (qrow(b, i), 1)),
                  pl.BlockSpec((IDX_HEADS, tq), lambda b, i: (0, qrow(b, i))),
                  pl.BlockSpec((None, klen, IDX_DIM), lambda b, i: (b, 0, 0)),
                  pl.BlockSpec((tq, AT_HEADS * KV_RANK), lambda b, i: (qrow(b, i), 0)),
                  pl.BlockSpec((None, klen, KV_RANK), lambda b, i: (b, 0, 0)),
                  pl.BlockSpec((None, KV_RANK, klen), lambda b, i: (b, 0, 0)),
                  pl.BlockSpec((AT_HEADS, AT_DH, KV_RANK), lambda b, i: (0, 0, 0)),
                  pl.BlockSpec((AT_HEADS, wn, tq), lambda b, i: (0, 0, 0))],
        out_specs=pl.BlockSpec((None, tq, AT_WIDTH), lambda b, i: (b, i, 0)),
        out_shape=jax.ShapeDtypeStruct((batch, n_tiles * tq, AT_WIDTH), BF16),
        scratch_shapes=[pltpu.VMEM((klen, tq), jnp.int32), pltpu.VMEM((2, klen, tq), F32)],
        compiler_params=_cparams(("parallel", "arbitrary")),
    )(p16, iw_t, ikn.reshape(batch, seq, IDX_DIM), qlat, ckvn.reshape(batch, seq, KV_RANK),
      ckvn_t, w_uv_t, near)


def _dsa(p16, iw_t, ikn, qlat, ckvn, ckvn_t, w_uv_t, near, batch, seq, tq):
    nq = seq // tq
    bounds = [0, 1] + list(range(2, nq + 1, 2))
    if bounds[-1] != nq:
        bounds.append(nq)
    outs = [_dsa_group(p16, iw_t, ikn, qlat, ckvn, ckvn_t, w_uv_t, near, batch, seq, tq, lo, hi - lo)
            for lo, hi in zip(bounds[:-1], bounds[1:])]
    return jnp.concatenate(outs, axis=1).reshape(batch * seq, AT_WIDTH)


def _xattn_kernel(q_ref, k_ref, v_ref, o_ref):
    scale = MEM_DH ** -0.5
    for h in range(MEM_HEADS):
        cols = slice(h * MEM_DH, (h + 1) * MEM_DH)
        lg = _dot_nt(q_ref[:, cols], k_ref[:, cols]) * scale
        p = jnp.exp(lg - jnp.max(lg, axis=-1, keepdims=True))
        p = p / jnp.sum(p, axis=-1, keepdims=True)
        o_ref[:, cols] = _dot(p.astype(BF16), v_ref[:, cols]).astype(o_ref.dtype)


def _xattn(q, k, v, batch, seq, mlen, tq):
    t = batch * seq
    nq = seq // tq
    return pl.pallas_call(
        _xattn_kernel,
        grid=(batch, nq),
        in_specs=[pl.BlockSpec((tq, D_MODEL), lambda b, i: (b * nq + i, 0)),
                  pl.BlockSpec((mlen, D_MODEL), lambda b, i: (b, 0)),
                  pl.BlockSpec((mlen, D_MODEL), lambda b, i: (b, 0))],
        out_specs=pl.BlockSpec((tq, D_MODEL), lambda b, i: (b * nq + i, 0)),
        out_shape=jax.ShapeDtypeStruct((t, D_MODEL), BF16),
        compiler_params=_cparams(("parallel", "parallel")),
    )(q, k, v)


def _top_values(x, k):
    rows = lax.broadcasted_iota(jnp.int32, (k, x.shape[1]), 0)
    out = jnp.zeros((k, x.shape[1]), F32)
    for a in range(k):
        mx = jnp.max(x, axis=0, keepdims=True)
        out = jnp.where(rows == a, mx, out)
        x = jnp.where(x == mx, NEG_INF, x)
    return out


def _peer_kernel(qp_ref, keys_ref, xt_ref, u_ref, vt_ref, res_ref, g_ref, b_ref, o_ref,
                 s_ref, e_ref, acc_ref, wg_ref, *, tt, te, tc):
    e = pl.program_id(1)
    nk = PEER_NKEYS
    k = PEER_TOPK

    @pl.when(e == 0)
    def _():
        acc_ref[...] = jnp.zeros_like(acc_ref)
        for hp in range(2 * PEER_HEADS):
            s_ref[hp] = _dot_nt(keys_ref[hp], qp_ref[:, hp * nk:(hp + 1) * nk])
        for h in range(PEER_HEADS):
            s0 = s_ref[2 * h]
            s1 = s_ref[2 * h + 1]
            v0 = _top_values(s0, k)
            v1 = _top_values(s1, k)
            half = k // 2
            cand = jnp.concatenate(
                [v0[0:1, :] + v1]
                + [v0[a:a + 1, :] + v1[0:half, :] for a in range(1, half)]
                + [v0[half:k, :] + v1[0:1, :]], axis=0)
            top = _top_values(cand, k)
            z = jnp.sum(jnp.exp(top - top[0:1, :]), axis=0, keepdims=True)
            theta = top[k - 1:k, :]
            e_ref[2 * h] = jnp.exp(s0 - v0[0:1, :]) / z
            e_ref[2 * h + 1] = jnp.exp(s1 - v1[0:1, :])
            cut = jnp.full(s0.shape, jnp.inf, F32)
            for a in range(k):
                v0a = v0[a:a + 1, :]
                cut_a = jnp.min(jnp.where(v0a + v1 >= theta, v1, jnp.inf), axis=0, keepdims=True)
                cut = jnp.where(s0 == v0a, cut_a, cut)
            s_ref[2 * h] = cut

    hts = [_dot(u_ref[c * tc:(c + 1) * tc, :], xt_ref[...]) for c in range(te // tc)]
    acc = acc_ref[...]
    for c in range(te // tc):
        rows = slice(c * tc, (c + 1) * tc)
        ht = hts[c]
        gel = 0.5 * ht * (1.0 + lax.erf(ht * (2.0 ** -0.5)))
        for ii in range(tc // nk):
            irow = e * (te // nk) + c * (tc // nk) + ii
            sub = slice(ii * nk, (ii + 1) * nk)
            cutrows = [s_ref[2 * h, pl.ds(irow, 1), :] for h in range(PEER_HEADS)]
            e0rows = [e_ref[2 * h, pl.ds(irow, 1), :] for h in range(PEER_HEADS)]
            for lc in range(tt // LANES):
                tok = slice(lc * LANES, (lc + 1) * LANES)
                w = jnp.zeros((nk, LANES), F32)
                for h in range(PEER_HEADS):
                    w = w + jnp.where(s_ref[2 * h + 1, :, tok] >= cutrows[h][:, tok],
                                      e0rows[h][:, tok] * e_ref[2 * h + 1, :, tok], 0.0)
                wg_ref[c * tc + ii * nk:c * tc + (ii + 1) * nk, tok] = (w * gel[sub, tok]).astype(BF16)
        acc = acc + _dot(vt_ref[:, rows], wg_ref[rows, :])
    acc_ref[...] = acc

    @pl.when(e == pl.num_programs(1) - 1)
    def _():
        for c in range(tt // tc):
            tok = slice(c * tc, (c + 1) * tc)
            y = ALPHA * res_ref[tok, :] + acc_ref[:, tok].T
            o_ref[tok, :] = _layer_norm_rows(y, g_ref[...], b_ref[...])


def _peer(qp, keys, h2, u, v_t, g, b, tt, te, tc):
    t, d = h2.shape
    n = u.shape[0]
    return pl.pallas_call(
        functools.partial(_peer_kernel, tt=tt, te=te, tc=tc),
        grid=(t // tt, n // te),
        in_specs=[pl.BlockSpec((tt, PEER_HEADS * PEER_DKEY), lambda i, e: (i, 0)),
                  pl.BlockSpec((2 * PEER_HEADS, PEER_NKEYS, PEER_DKEY // 2), lambda i, e: (0, 0, 0)),
                  pl.BlockSpec((d, tt), lambda i, e: (0, i)),
                  pl.BlockSpec((te, d), lambda i, e: (e, 0)),
                  pl.BlockSpec((d, te), lambda i, e: (0, e)),
                  pl.BlockSpec((tt, d), lambda i, e: (i, 0)),
                  pl.BlockSpec((1, d), lambda i, e: (0, 0)),
                  pl.BlockSpec((1, d), lambda i, e: (0, 0))],
        out_specs=pl.BlockSpec((tt, d), lambda i, e: (i, 0)),
        out_shape=jax.ShapeDtypeStruct((t, d), F32),
        scratch_shapes=[pltpu.VMEM((2 * PEER_HEADS, PEER_NKEYS, tt), F32),
                        pltpu.VMEM((2 * PEER_HEADS, PEER_NKEYS, tt), F32),
                        pltpu.VMEM((d, tt), F32),
                        pltpu.VMEM((te, tt), BF16)],
        compiler_params=_cparams(("parallel", "arbitrary")),
    )(qp, keys, h2.astype(BF16).T, u, v_t, h2, g.reshape(1, d), b.reshape(1, d))


def kernel(x, mem, w_in, lb_logits, hg_norm_g, kv_norm_g, ik_norm_g, ik_norm_b, w_uk, w_uv,
           rel_bias, w_branch_a, w_branch_b, w_mix_out, ln1_g, ln1_b, w_cq, w_ck, w_cv, w_co,
           ln2_g, ln2_b, w_pq, peer_keys, peer_u, peer_v, ln3_g, ln3_b):
    batch, seq, d = x.shape
    mlen = mem.shape[1]
    t = batch * seq
    l = 0

    wi = w_in[l]
    o_aq = 4 * HG_WIDTH
    o_ckv = o_aq + AT_WIDTH
    o_iq = o_ckv + KV_RANK
    o_ik = o_iq + IDX_HEADS * IDX_DIM
    o_iw = o_ik + IDX_DIM
    o_ga = o_iw + IDX_HEADS
    w32 = jnp.concatenate([wi[:, :o_aq], wi[:, o_ckv:o_iq], wi[:, o_ik:o_iw],
                           jnp.pad(wi[:, o_iw:o_ga], ((0, 0), (0, LANES - IDX_HEADS))),
                           wi[:, o_ga:]], axis=1).astype(BF16)
    w16 = jnp.concatenate([wi[:, o_aq:o_ckv], wi[:, o_iq:o_ik]], axis=1).astype(BF16)
    xf = x.reshape(t, d)
    xb = xf.astype(BF16)

    p32 = _matmul(xb, w32, 2048, 512, F32)
    p16 = _matmul(xb, w16, 2048, 1024, BF16)
    ya_in = _hgrn(p32, lb_logits, hg_norm_g[l], batch, seq)
    ckvn, ikn, qlat = _dsa_prep(p32, p16, kv_norm_g[l], ik_norm_g[l], ik_norm_b[l],
                                w_uk[l].astype(BF16), 1024)
    tq = 256
    iw_t = p32[:, C_IW:C_IW + IDX_HEADS].T
    ckvn_t = jnp.swapaxes(ckvn.reshape(batch, seq, KV_RANK), 1, 2)
    w_uv_t = jnp.swapaxes(w_uv[l], 1, 2).astype(BF16)
    near = _near_bias_table(rel_bias, tq)
    yb_in = _dsa(p16, iw_t, ikn, qlat, ckvn, ckvn_t, w_uv_t, near, batch, seq, tq)
    mix_in = _merge(ya_in, yb_in, w_branch_a[l].astype(BF16), w_branch_b[l].astype(BF16), p32, 1024, 512)
    h1 = _matmul_res_ln(mix_in, w_mix_out[l].astype(BF16), xf, ln1_g[l], ln1_b[l], 512)

    qc = _matmul(h1, w_cq[l].astype(BF16), 1024, 1024, BF16)
    memf = mem.reshape(batch * mlen, d).astype(BF16)
    kc = _matmul(memf, w_ck[l].astype(BF16), 1024, 1024, BF16)
    vc = _matmul(memf, w_cv[l].astype(BF16), 1024, 1024, BF16)
    ca_in = _xattn(qc, kc, vc, batch, seq, mlen, 512)
    h2 = _matmul_res_ln(ca_in, w_co[l].astype(BF16), h1, ln2_g[l], ln2_b[l], 512)

    qp = _matmul(h2, w_pq[l].astype(BF16), 1024, 1024, BF16)
    keys = peer_keys[l].reshape(2 * PEER_HEADS, PEER_NKEYS, PEER_DKEY // 2).astype(BF16)
    u = peer_u[l].astype(BF16)
    v_t = peer_v[l].astype(BF16).T
    h3 = _peer(qp, keys, h2, u, v_t, ln3_g[l], ln3_b[l], 512, 512, 256)
    return h3.reshape(batch, seq, d)
```

```python
import functools
import math

import jax
import jax.numpy as jnp
from jax import lax
from jax.experimental import pallas as pl
from jax.experimental.pallas import tpu as pltpu

F32 = jnp.float32
BF16 = jnp.bfloat16

D_MODEL = 2048
ALPHA = 2.0 ** 0.25
LN_EPS = 1e-5
RMS_EPS = 1e-6
HG_DK = 128
HG_HEADS = D_MODEL // 128
HG_WIDTH = HG_HEADS * HG_DK
HG_CHUNK = 64
AT_DH = 128
AT_HEADS = D_MODEL // 128
AT_WIDTH = AT_HEADS * AT_DH
KV_RANK = D_MODEL // 8
IDX_HEADS = 16
IDX_DIM = 128
TOPK_MAX = 256
REL_BUCKETS = 32
REL_MAX_DIST = 128
MEM_HEADS = 4
MEM_DH = D_MODEL // MEM_HEADS
PEER_HEADS = 8
PEER_NKEYS = 128
PEER_N = PEER_NKEYS * PEER_NKEYS
PEER_DKEY = 256
PEER_TOPK = 16

LANES = 128
VMEM_LIMIT = 56 * 1024 * 1024

C_HQ, C_HF, C_HI, C_HG = 0, HG_WIDTH, 2 * HG_WIDTH, 3 * HG_WIDTH
C_CKV = 4 * HG_WIDTH
C_IK = C_CKV + KV_RANK
C_IW = C_IK + IDX_DIM
C_GA = C_IW + LANES
C_GB = C_GA + D_MODEL
N_P32 = C_GB + D_MODEL

NEG_INF = float("-inf")


def _cparams(sem):
    return pltpu.CompilerParams(dimension_semantics=sem, vmem_limit_bytes=VMEM_LIMIT)


def _dot(a, b):
    return jnp.dot(a, b, preferred_element_type=F32)


def _dot_nt(a, b):
    return lax.dot_general(a, b, (((1,), (1,)), ((), ())), preferred_element_type=F32)


def _dot_tn(a, b):
    return lax.dot_general(a, b, (((0,), (0,)), ((), ())), preferred_element_type=F32)


def _mm_kernel(a_ref, w_ref, o_ref):
    o_ref[...] = _dot(a_ref[...].astype(BF16), w_ref[...]).astype(o_ref.dtype)


def _matmul(a, w, tm, tn, out_dtype):
    m, k = a.shape
    n = w.shape[1]
    return pl.pallas_call(
        _mm_kernel,
        grid=(m // tm, n // tn),
        in_specs=[pl.BlockSpec((tm, k), lambda i, j: (i, 0)),
                  pl.BlockSpec((k, tn), lambda i, j: (0, j))],
        out_specs=pl.BlockSpec((tm, tn), lambda i, j: (i, j)),
        out_shape=jax.ShapeDtypeStruct((m, n), out_dtype),
        compiler_params=_cparams(("parallel", "parallel")),
    )(a, w)


def _layer_norm_rows(y, g, b):
    mu = jnp.mean(y, axis=-1, keepdims=True)
    yc = y - mu
    var = jnp.mean(yc * yc, axis=-1, keepdims=True)
    return yc * lax.rsqrt(var + LN_EPS) * g + b


def _mm_res_ln_kernel(a_ref, w_ref, res_ref, g_ref, b_ref, o_ref):
    y = ALPHA * res_ref[...] + _dot(a_ref[...].astype(BF16), w_ref[...])
    o_ref[...] = _layer_norm_rows(y, g_ref[...], b_ref[...])


def _matmul_res_ln(a, w, res, g, b, tm):
    m, k = a.shape
    n = w.shape[1]
    return pl.pallas_call(
        _mm_res_ln_kernel,
        grid=(m // tm,),
        in_specs=[pl.BlockSpec((tm, k), lambda i: (i, 0)),
                  pl.BlockSpec((k, n), lambda i: (0, 0)),
                  pl.BlockSpec((tm, n), lambda i: (i, 0)),
                  pl.BlockSpec((1, n), lambda i: (0, 0)),
                  pl.BlockSpec((1, n), lambda i: (0, 0))],
        out_specs=pl.BlockSpec((tm, n), lambda i: (i, 0)),
        out_shape=jax.ShapeDtypeStruct((m, n), F32),
        compiler_params=_cparams(("parallel",)),
    )(a, w, res, g.reshape(1, n), b.reshape(1, n))


def _merge_kernel(a1_ref, a2_ref, w1_ref, w2_ref, ga_ref, gb_ref, o_ref):
    ya = _dot(a1_ref[...], w1_ref[...])
    yb = _dot(a2_ref[...], w2_ref[...])
    o_ref[...] = (jax.nn.sigmoid(ga_ref[...]) * ya + jax.nn.sigmoid(gb_ref[...]) * yb).astype(o_ref.dtype)


def _merge(a1, a2, w1, w2, p32, tm, tn):
    m, k = a1.shape
    n = w1.shape[1]
    ga_blk, gb_blk = C_GA // tn, C_GB // tn
    return pl.pallas_call(
        _merge_kernel,
        grid=(m // tm, n // tn),
        in_specs=[pl.BlockSpec((tm, k), lambda i, j: (i, 0)),
                  pl.BlockSpec((tm, k), lambda i, j: (i, 0)),
                  pl.BlockSpec((k, tn), lambda i, j: (0, j)),
                  pl.BlockSpec((k, tn), lambda i, j: (0, j)),
                  pl.BlockSpec((tm, tn), lambda i, j: (i, ga_blk + j)),
                  pl.BlockSpec((tm, tn), lambda i, j: (i, gb_blk + j))],
        out_specs=pl.BlockSpec((tm, tn), lambda i, j: (i, j)),
        out_shape=jax.ShapeDtypeStruct((m, n), BF16),
        compiler_params=_cparams(("parallel", "parallel")),
    )(a1, a2, w1, w2, p32, p32)


def _hgrn_kernel(q_ref, f_ref, i_ref, g_ref, lb_ref, ng_ref, o_ref, st_ref, *, hb, rb):
    c = HG_CHUNK

    @pl.when(pl.program_id(2) == 0)
    def _():
        st_ref[...] = jnp.zeros_like(st_ref)

    l0 = lb_ref[0:1, :]
    l1 = lb_ref[1:2, :]
    mx = jnp.maximum(l0, l1)
    e0 = jnp.exp(l0 - mx)
    lb = e0 / (e0 + jnp.exp(l1 - mx))

    f = lb + (1.0 - lb) * jax.nn.sigmoid(f_ref[...])
    kk = 1.0 - f
    gcum = jnp.log(f)
    pos = lax.broadcasted_iota(jnp.int32, gcum.shape, 0) & (c - 1)
    for s in (1, 2, 4, 8, 16, 32):
        gcum = gcum + jnp.where(pos >= s, pltpu.roll(gcum, s, axis=0), 0.0)

    tril = (lax.broadcasted_iota(jnp.int32, (c, c), 0) >= lax.broadcasted_iota(jnp.int32, (c, c), 1))
    ng = ng_ref[...]
    for ci in range(rb // c):
        rows = slice(ci * c, (ci + 1) * c)
        for h in range(hb):
            cols = slice(h * HG_DK, (h + 1) * HG_DK)
            gc = gcum[rows, cols]
            g_ref_row = gc[c // 2 - 1:c // 2, :]
            g_last = gc[c - 1:c, :]
            qc = q_ref[rows, cols]
            kc = kk[rows, cols]
            vc = i_ref[rows, cols].astype(BF16)
            qa = (qc * jnp.exp(gc - g_ref_row)).astype(BF16)
            kb = (kc * jnp.exp(g_ref_row - gc)).astype(BF16)
            a = jnp.where(tril, _dot_nt(qa, kb), 0.0)
            o = _dot(a.astype(BF16), vc)
            st = st_ref[h]
            qd = (qc * jnp.exp(gc)).astype(BF16)
            o = o + _dot_nt(qd, st.astype(BF16))
            kd = (kc * jnp.exp(g_last - gc)).astype(BF16)
            st_ref[h] = st * jnp.exp(g_last) + _dot_tn(vc, kd)
            o = o * lax.rsqrt(jnp.mean(o * o, axis=-1, keepdims=True) + RMS_EPS) * ng[:, cols]
            gate = g_ref[rows, cols]
            o_ref[rows, cols] = (o * (gate * jax.nn.sigmoid(gate))).astype(o_ref.dtype)


def _hgrn(p32, lb_logits, norm_g, batch, seq, hb=4, rb=512):
    t = batch * seq
    w = hb * HG_DK
    nr = seq // rb
    hq, hf, hi, hg = (cc // w for cc in (C_HQ, C_HF, C_HI, C_HG))
    row = lambda b, h, r: b * nr + r
    in_spec = lambda off: pl.BlockSpec((rb, w), lambda b, h, r: (row(b, h, r), off + h))
    return pl.pallas_call(
        functools.partial(_hgrn_kernel, hb=hb, rb=rb),
        grid=(batch, HG_HEADS // hb, nr),
        in_specs=[in_spec(hq), in_spec(hf), in_spec(hi), in_spec(hg),
                  pl.BlockSpec((2, w), lambda b, h, r: (0, h)),
                  pl.BlockSpec((1, w), lambda b, h, r: (0, h))],
        out_specs=pl.BlockSpec((rb, w), lambda b, h, r: (row(b, h, r), h)),
        out_shape=jax.ShapeDtypeStruct((t, HG_WIDTH), BF16),
        scratch_shapes=[pltpu.VMEM((hb, HG_DK, HG_DK), F32)],
        compiler_params=_cparams(("parallel", "parallel", "arbitrary")),
    )(p32, p32, p32, p32, lb_logits, norm_g.reshape(1, HG_WIDTH))


def _dsa_prep_kernel(ckv_ref, ik_ref, aq_ref, kvg_ref, ikg_ref, ikb_ref, wuk_ref,
                     ckvn_ref, ikn_ref, qlat_ref):
    ckv = ckv_ref[...]
    ckvn_ref[...] = (ckv * lax.rsqrt(jnp.mean(ckv * ckv, axis=-1, keepdims=True) + RMS_EPS)
                     * kvg_ref[...]).astype(ckvn_ref.dtype)
    ikn_ref[...] = _layer_norm_rows(ik_ref[...], ikg_ref[...], ikb_ref[...]).astype(ikn_ref.dtype)
    scale = AT_DH ** -0.5
    for h in range(AT_HEADS):
        qh = aq_ref[:, h * AT_DH:(h + 1) * AT_DH]
        qlat_ref[:, h * KV_RANK:(h + 1) * KV_RANK] = (_dot_nt(qh, wuk_ref[h]) * scale).astype(qlat_ref.dtype)


def _dsa_prep(p32, p16, kv_g, ik_g, ik_b, w_uk, tm):
    t = p32.shape[0]
    return pl.pallas_call(
        _dsa_prep_kernel,
        grid=(t // tm,),
        in_specs=[pl.BlockSpec((tm, KV_RANK), lambda i: (i, C_CKV // KV_RANK)),
                  pl.BlockSpec((tm, IDX_DIM), lambda i: (i, C_IK // IDX_DIM)),
                  pl.BlockSpec((tm, AT_WIDTH), lambda i: (i, 0)),
                  pl.BlockSpec((1, KV_RANK), lambda i: (0, 0)),
                  pl.BlockSpec((1, IDX_DIM), lambda i: (0, 0)),
                  pl.BlockSpec((1, IDX_DIM), lambda i: (0, 0)),
                  pl.BlockSpec((AT_HEADS, KV_RANK, AT_DH), lambda i: (0, 0, 0))],
        out_specs=[pl.BlockSpec((tm, KV_RANK), lambda i: (i, 0)),
                   pl.BlockSpec((tm, IDX_DIM), lambda i: (i, 0)),
                   pl.BlockSpec((tm, AT_HEADS * KV_RANK), lambda i: (i, 0))],
        out_shape=[jax.ShapeDtypeStruct((t, KV_RANK), BF16),
                   jax.ShapeDtypeStruct((t, IDX_DIM), BF16),
                   jax.ShapeDtypeStruct((t, AT_HEADS * KV_RANK), BF16)],
        compiler_params=_cparams(("parallel",)),
    )(p32, p32, p16, kv_g.reshape(1, -1), ik_g.reshape(1, -1), ik_b.reshape(1, -1), w_uk)


def _dsa_kernel(iq_ref, iwt_ref, ik_ref, qlat_ref, ckv_ref, ckvt_ref, wuvt_ref, near_ref,
                o_ref, key_ref, lg_ref, *, tq, seq, topk, tile_lo):
    i = tile_lo + pl.program_id(1)
    t0 = i * tq
    ik = ik_ref[...]
    w_scale = (IDX_DIM ** -0.5) * (IDX_HEADS ** -0.5)
    sc = jnp.zeros((seq, tq), F32)
    for h in range(IDX_HEADS):
        s = _dot_nt(ik, iq_ref[:, h * IDX_DIM:(h + 1) * IDX_DIM])
        sc = sc + jnp.maximum(s, 0.0) * (iwt_ref[h:h + 1, :] * w_scale)
    kpos = lax.broadcasted_iota(jnp.int32, (seq, tq), 0)
    tpos = t0 + lax.broadcasted_iota(jnp.int32, (seq, tq), 1)
    causal = kpos <= tpos
    sc = jnp.where(causal, sc + 0.0, NEG_INF)
    bits = pltpu.bitcast(sc, jnp.int32)
    key_ref[...] = bits ^ ((bits >> 31) & jnp.int32(0x7FFFFFFF))

    def bit_step(it, thr):
        cand = thr + jnp.left_shift(jnp.int32(1), 31 - it)
        cnt = jnp.sum((key_ref[...] >= cand).astype(F32), axis=0, keepdims=True)
        return jnp.where(cnt >= float(topk), cand, thr)

    thr = lax.fori_loop(0, 32, bit_step, jnp.full((1, tq), jnp.iinfo(jnp.int32).min, jnp.int32))
    addmask = jnp.where((key_ref[...] >= thr) & causal, 0.0, NEG_INF)

    wn = tq + REL_MAX_DIST
    for h in range(AT_HEADS):
        lg_h = lg_ref.at[h % 2]
        lg_h[...] = _dot_nt(ckv_ref[...], qlat_ref[:, h * KV_RANK:(h + 1) * KV_RANK]) + addmask
        if tile_lo == 0:
            lg_h[0:tq, :] += near_ref[h, REL_MAX_DIST:wn, :]
        else:
            start = pl.multiple_of(t0 - REL_MAX_DIST, REL_MAX_DIST)
            lg_h[pl.ds(start, wn), :] += near_ref[h]
        lg = lg_h[...]
        p = jnp.exp(lg - jnp.max(lg, axis=0, keepdims=True))
        denom = jnp.sum(p, axis=0, keepdims=True)
        olat = _dot(ckvt_ref[...], p.astype(BF16)) / denom
        out_t = _dot(wuvt_ref[h], olat.astype(BF16))
        o_ref[:, h * AT_DH:(h + 1) * AT_DH] = out_t.T.astype(o_ref.dtype)


def _near_bias_table(rel_bias, tq):
    wn = tq + REL_MAX_DIST
    d = (jnp.arange(tq, dtype=jnp.int32)[None, :] - jnp.arange(wn, dtype=jnp.int32)[:, None]
         + REL_MAX_DIST)
    max_exact = REL_BUCKETS // 2
    dd = jnp.maximum(d, 0)
    large = max_exact + (jnp.log(jnp.maximum(dd, 1).astype(F32) / max_exact)
                         / math.log(REL_MAX_DIST / max_exact) * (REL_BUCKETS - max_exact)).astype(jnp.int32)
    large = jnp.minimum(large, REL_BUCKETS - 1)
    bucket = jnp.where(dd < max_exact, dd, large)
    rb = rel_bias.astype(F32) - rel_bias.astype(F32)[REL_BUCKETS - 1]
    onehot = (bucket[None, :, :] == jnp.arange(REL_BUCKETS, dtype=jnp.int32)[:, None, None]).astype(F32)
    return jnp.einsum("bh,bcr->hcr", rb, onehot, precision=lax.Precision.HIGHEST)


def _dsa_group(p16, iw_t, ikn, qlat, ckvn, ckvn_t, w_uv_t, near, batch, seq, tq, tile_lo, n_tiles):
    nq = seq // tq
    klen = (tile_lo + n_tiles) * tq
    topk = min(TOPK_MAX, seq // 4)
    wn = tq + REL_MAX_DIST
    qrow = lambda b, i: b * nq + tile_lo + i
    return pl.pallas_call(
        functools.partial(_dsa_kernel, tq=tq, seq=klen, topk=topk, tile_lo=tile_lo),
        grid=(batch, n_tiles),
        in_specs=[pl.BlockSpec((tq, IDX_HEADS * IDX_DIM), lambda b, i: (qrow(b, i), 1)),
                  pl.BlockSpec((IDX_HEADS, tq), lambda b, i: (0, qrow(b, i))),
                  pl.BlockSpec((None, klen, IDX_DIM), lambda b, i: (b, 0, 0)),
                  pl.BlockSpec((tq, AT_HEADS * KV_RANK), lambda b, i: (qrow(b, i), 0)),
                  pl.BlockSpec((None, klen, KV_RANK), lambda b, i: (b, 0, 0)),
                  pl.BlockSpec((None, KV_RANK, klen), lambda b, i: (b, 0, 0)),
                  pl.BlockSpec((AT_HEADS, AT_DH, KV_RANK), lambda b, i: (0, 0, 0)),
                  pl.BlockSpec((AT_HEADS, wn, tq), lambda b, i: (0, 0, 0))],
        out_specs=pl.BlockSpec((None, tq, AT_WIDTH), lambda b, i: (b, i, 0)),
        out_shape=jax.ShapeDtypeStruct((batch, n_tiles * tq, AT_WIDTH), BF16),
        scratch_shapes=[pltpu.VMEM((klen, tq), jnp.int32), pltpu.VMEM((2, klen, tq), F32)],
        compiler_params=_cparams(("parallel", "arbitrary")),
    )(p16, iw_t, ikn.reshape(batch, seq, IDX_DIM), qlat, ckvn.reshape(batch, seq, KV_RANK),
      ckvn_t, w_uv_t, near)


def _dsa(p16, iw_t, ikn, qlat, ckvn, ckvn_t, w_uv_t, near, batch, seq, tq):
    nq = seq // tq
    bounds = [0, 1] + list(range(2, nq + 1, 2))
    if bounds[-1] != nq:
        bounds.append(nq)
    outs = [_dsa_group(p16, iw_t, ikn, qlat, ckvn, ckvn_t, w_uv_t, near, batch, seq, tq, lo, hi - lo)
            for lo, hi in zip(bounds[:-1], bounds[1:])]
    return jnp.concatenate(outs, axis=1).reshape(batch * seq, AT_WIDTH)


def _xattn_kernel(q_ref, k_ref, v_ref, o_ref):
    scale = MEM_DH ** -0.5
    for h in range(MEM_HEADS):
        cols = slice(h * MEM_DH, (h + 1) * MEM_DH)
        lg = _dot_nt(q_ref[:, cols], k_ref[:, cols]) * scale
        p = jnp.exp(lg - jnp.max(lg, axis=-1, keepdims=True))
        p = p / jnp.sum(p, axis=-1, keepdims=True)
        o_ref[:, cols] = _dot(p.astype(BF16), v_ref[:, cols]).astype(o_ref.dtype)


def _xattn(q, k, v, batch, seq, mlen, tq):
    t = batch * seq
    nq = seq // tq
    return pl.pallas_call(
        _xattn_kernel,
        grid=(batch, nq),
        in_specs=[pl.BlockSpec((tq, D_MODEL), lambda b, i: (b * nq + i, 0)),
                  pl.BlockSpec((mlen, D_MODEL), lambda b, i: (b, 0)),
                  pl.BlockSpec((mlen, D_MODEL), lambda b, i: (b, 0))],
        out_specs=pl.BlockSpec((tq, D_MODEL), lambda b, i: (b * nq + i, 0)),
        out_shape=jax.ShapeDtypeStruct((t, D_MODEL), BF16),
        compiler_params=_cparams(("parallel", "parallel")),
    )(q, k, v)


def _top_values(x, k):
    rows = lax.broadcasted_iota(jnp.int32, (k, x.shape[1]), 0)
    out = jnp.zeros((k, x.shape[1]), F32)
    for a in range(k):
        mx = jnp.max(x, axis=0, keepdims=True)
        out = jnp.where(rows == a, mx, out)
        x = jnp.where(x == mx, NEG_INF, x)
    return out


def _peer_kernel(qp_ref, keys_ref, xt_ref, u_ref, vt_ref, res_ref, g_ref, b_ref, o_ref,
                 s_ref, e_ref, acc_ref, wg_ref, *, tt, te, tc):
    e = pl.program_id(1)
    nk = PEER_NKEYS
    k = PEER_TOPK

    @pl.when(e == 0)
    def _():
        acc_ref[...] = jnp.zeros_like(acc_ref)
        for hp in range(2 * PEER_HEADS):
            s_ref[hp] = _dot_nt(keys_ref[hp], qp_ref[:, hp * nk:(hp + 1) * nk])
        for h in range(PEER_HEADS):
            s0 = s_ref[2 * h]
            s1 = s_ref[2 * h + 1]
            v0 = _top_values(s0, k)
            v1 = _top_values(s1, k)
            half = k // 2
            cand = jnp.concatenate(
                [v0[0:1, :] + v1]
                + [v0[a:a + 1, :] + v1[0:half, :] for a in range(1, half)]
                + [v0[half:k, :] + v1[0:1, :]], axis=0)
            top = _top_values(cand, k)
            z = jnp.sum(jnp.exp(top - top[0:1, :]), axis=0, keepdims=True)
            theta = top[k - 1:k, :]
            e_ref[2 * h] = jnp.exp(s0 - v0[0:1, :]) * (0.5 / z)
            e_ref[2 * h + 1] = jnp.exp(s1 - v1[0:1, :])
            cut = jnp.full(s0.shape, jnp.inf, F32)
            for a in range(k):
                v0a = v0[a:a + 1, :]
                cut_a = jnp.min(jnp.where(v0a + v1 >= theta, v1, jnp.inf), axis=0, keepdims=True)
                cut = jnp.where(s0 == v0a, cut_a, cut)
            s_ref[2 * h] = cut

    hts = [_dot(u_ref[c * tc:(c + 1) * tc, :], xt_ref[...]) for c in range(te // tc)]
    acc = acc_ref[...]
    for c in range(te // tc):
        rows = slice(c * tc, (c + 1) * tc)
        ht = hts[c]
        gel = ht * (1.0 + lax.erf(ht * (2.0 ** -0.5)))
        for ii in range(tc // nk):
            irow = e * (te // nk) + c * (tc // nk) + ii
            sub = slice(ii * nk, (ii + 1) * nk)
            cutrows = [s_ref[2 * h, pl.ds(irow, 1), :] for h in range(PEER_HEADS)]
            e0rows = [e_ref[2 * h, pl.ds(irow, 1), :] for h in range(PEER_HEADS)]
            for lc in range(tt // LANES):
                tok = slice(lc * LANES, (lc + 1) * LANES)
                w = None
                for h in range(PEER_HEADS):
                    wh = jnp.where(s_ref[2 * h + 1, :, tok] >= cutrows[h][:, tok],
                                   e0rows[h][:, tok] * e_ref[2 * h + 1, :, tok], 0.0)
                    w = wh if w is None else w + wh
                wg_ref[c * tc + ii * nk:c * tc + (ii + 1) * nk, tok] = (w * gel[sub, tok]).astype(BF16)
        acc = acc + _dot(vt_ref[:, rows], wg_ref[rows, :])
    acc_ref[...] = acc

    @pl.when(e == pl.num_programs(1) - 1)
    def _():
        for c in range(tt // tc):
            tok = slice(c * tc, (c + 1) * tc)
            y = ALPHA * res_ref[tok, :] + acc_ref[:, tok].T
            o_ref[tok, :] = _layer_norm_rows(y, g_ref[...], b_ref[...])


def _peer(qp, keys, h2, u, v_t, g, b, tt, te, tc):
    t, d = h2.shape
    n = u.shape[0]
    return pl.pallas_call(
        functools.partial(_peer_kernel, tt=tt, te=te, tc=tc),
        grid=(t // tt, n // te),
        in_specs=[pl.BlockSpec((tt, PEER_HEADS * PEER_DKEY), lambda i, e: (i, 0)),
                  pl.BlockSpec((2 * PEER_HEADS, PEER_NKEYS, PEER_DKEY // 2), lambda i, e: (0, 0, 0)),
                  pl.BlockSpec((d, tt), lambda i, e: (0, i)),
                  pl.BlockSpec((te, d), lambda i, e: (e, 0)),
                  pl.BlockSpec((d, te), lambda i, e: (0, e)),
                  pl.BlockSpec((tt, d), lambda i, e: (i, 0)),
                  pl.BlockSpec((1, d), lambda i, e: (0, 0)),
                  pl.BlockSpec((1, d), lambda i, e: (0, 0))],
        out_specs=pl.BlockSpec((tt, d), lambda i, e: (i, 0)),
        out_shape=jax.ShapeDtypeStruct((t, d), F32),
        scratch_shapes=[pltpu.VMEM((2 * PEER_HEADS, PEER_NKEYS, tt), F32),
                        pltpu.VMEM((2 * PEER_HEADS, PEER_NKEYS, tt), F32),
                        pltpu.VMEM((d, tt), F32),
                        pltpu.VMEM((te, tt), BF16)],
        compiler_params=_cparams(("parallel", "arbitrary")),
    )(qp, keys, h2.astype(BF16).T, u, v_t, h2, g.reshape(1, d), b.reshape(1, d))


def kernel(x, mem, w_in, lb_logits, hg_norm_g, kv_norm_g, ik_norm_g, ik_norm_b, w_uk, w_uv,
           rel_bias, w_branch_a, w_branch_b, w_mix_out, ln1_g, ln1_b, w_cq, w_ck, w_cv, w_co,
           ln2_g, ln2_b, w_pq, peer_keys, peer_u, peer_v, ln3_g, ln3_b):
    batch, seq, d = x.shape
    mlen = mem.shape[1]
    t = batch * seq
    l = 0

    wi = w_in[l]
    o_aq = 4 * HG_WIDTH
    o_ckv = o_aq + AT_WIDTH
    o_iq = o_ckv + KV_RANK
    o_ik = o_iq + IDX_HEADS * IDX_DIM
    o_iw = o_ik + IDX_DIM
    o_ga = o_iw + IDX_HEADS
    w32 = jnp.concatenate([wi[:, :o_aq], wi[:, o_ckv:o_iq], wi[:, o_ik:o_iw],
                           jnp.pad(wi[:, o_iw:o_ga], ((0, 0), (0, LANES - IDX_HEADS))),
                           wi[:, o_ga:]], axis=1).astype(BF16)
    w16 = jnp.concatenate([wi[:, o_aq:o_ckv], wi[:, o_iq:o_ik]], axis=1).astype(BF16)
    xf = x.reshape(t, d)
    xb = xf.astype(BF16)

    p32 = _matmul(xb, w32, 2048, 512, F32)
    p16 = _matmul(xb, w16, 2048, 1024, BF16)
    ya_in = _hgrn(p32, lb_logits, hg_norm_g[l], batch, seq)
    ckvn, ikn, qlat = _dsa_prep(p32, p16, kv_norm_g[l], ik_norm_g[l], ik_norm_b[l],
                                w_uk[l].astype(BF16), 1024)
    tq = 256
    iw_t = p32[:, C_IW:C_IW + IDX_HEADS].T
    ckvn_t = jnp.swapaxes(ckvn.reshape(batch, seq, KV_RANK), 1, 2)
    w_uv_t = jnp.swapaxes(w_uv[l], 1, 2).astype(BF16)
    near = _near_bias_table(rel_bias, tq)
    yb_in = _dsa(p16, iw_t, ikn, qlat, ckvn, ckvn_t, w_uv_t, near, batch, seq, tq)
    mix_in = _merge(ya_in, yb_in, w_branch_a[l].astype(BF16), w_branch_b[l].astype(BF16), p32, 1024, 512)
    h1 = _matmul_res_ln(mix_in, w_mix_out[l].astype(BF16), xf, ln1_g[l], ln1_b[l], 512)

    qc = _matmul(h1, w_cq[l].astype(BF16), 1024, 1024, BF16)
    memf = mem.reshape(batch * mlen, d).astype(BF16)
    kc = _matmul(memf, w_ck[l].astype(BF16), 1024, 1024, BF16)
    vc = _matmul(memf, w_cv[l].astype(BF16), 1024, 1024, BF16)
    ca_in = _xattn(qc, kc, vc, batch, seq, mlen, 512)
    h2 = _matmul_res_ln(ca_in, w_co[l].astype(BF16), h1, ln2_g[l], ln2_b[l], 512)

    qp = _matmul(h2, w_pq[l].astype(BF16), 1024, 1024, BF16)
    keys = peer_keys[l].reshape(2 * PEER_HEADS, PEER_NKEYS, PEER_DKEY // 2).astype(BF16)
    u = peer_u[l].astype(BF16)
    v_t = peer_v[l].astype(BF16).T
    h3 = _peer(qp, keys, h2, u, v_t, ln3_g[l], ln3_b[l], 512, 512, 256)
    return h3.reshape(batch, seq, d)
```

```python
import functools
import math

import jax
import jax.numpy as jnp
from jax import lax
from jax.experimental import pallas as pl
from jax.experimental.pallas import tpu as pltpu

F32 = jnp.float32
BF16 = jnp.bfloat16

D_MODEL = 2048
ALPHA = 2.0 ** 0.25
LN_EPS = 1e-5
RMS_EPS = 1e-6
HG_DK = 128
HG_HEADS = D_MODEL // 128
HG_WIDTH = HG_HEADS * HG_DK
HG_CHUNK = 64
AT_DH = 128
AT_HEADS = D_MODEL // 128
AT_WIDTH = AT_HEADS * AT_DH
KV_RANK = D_MODEL // 8
IDX_HEADS = 16
IDX_DIM = 128
TOPK_MAX = 256
REL_BUCKETS = 32
REL_MAX_DIST = 128
MEM_HEADS = 4
MEM_DH = D_MODEL // MEM_HEADS
PEER_HEADS = 8
PEER_NKEYS = 128
PEER_N = PEER_NKEYS * PEER_NKEYS
PEER_DKEY = 256
PEER_TOPK = 16

LANES = 128
VMEM_LIMIT = 56 * 1024 * 1024

C_HQ, C_HF, C_HI, C_HG = 0, HG_WIDTH, 2 * HG_WIDTH, 3 * HG_WIDTH
C_CKV = 4 * HG_WIDTH
C_IK = C_CKV + KV_RANK
C_IW = C_IK + IDX_DIM
C_GA = C_IW + LANES
C_GB = C_GA + D_MODEL
N_P32 = C_GB + D_MODEL

NEG_INF = float("-inf")


def _cparams(sem):
    return pltpu.CompilerParams(dimension_semantics=sem, vmem_limit_bytes=VMEM_LIMIT)


def _dot(a, b):
    return jnp.dot(a, b, preferred_element_type=F32)


def _dot_nt(a, b):
    return lax.dot_general(a, b, (((1,), (1,)), ((), ())), preferred_element_type=F32)


def _dot_tn(a, b):
    return lax.dot_general(a, b, (((0,), (0,)), ((), ())), preferred_element_type=F32)


def _mm_kernel(a_ref, w_ref, o_ref):
    o_ref[...] = _dot(a_ref[...].astype(BF16), w_ref[...]).astype(o_ref.dtype)


def _matmul(a, w, tm, tn, out_dtype):
    m, k = a.shape
    n = w.shape[1]
    return pl.pallas_call(
        _mm_kernel,
        grid=(m // tm, n // tn),
        in_specs=[pl.BlockSpec((tm, k), lambda i, j: (i, 0)),
                  pl.BlockSpec((k, tn), lambda i, j: (0, j))],
        out_specs=pl.BlockSpec((tm, tn), lambda i, j: (i, j)),
        out_shape=jax.ShapeDtypeStruct((m, n), out_dtype),
        compiler_params=_cparams(("parallel", "parallel")),
    )(a, w)


def _layer_norm_rows(y, g, b):
    mu = jnp.mean(y, axis=-1, keepdims=True)
    yc = y - mu
    var = jnp.mean(yc * yc, axis=-1, keepdims=True)
    return yc * lax.rsqrt(var + LN_EPS) * g + b


def _mm_res_ln_kernel(a_ref, w_ref, res_ref, g_ref, b_ref, o_ref):
    y = ALPHA * res_ref[...] + _dot(a_ref[...].astype(BF16), w_ref[...])
    o_ref[...] = _layer_norm_rows(y, g_ref[...], b_ref[...])


def _matmul_res_ln(a, w, res, g, b, tm):
    m, k = a.shape
    n = w.shape[1]
    return pl.pallas_call(
        _mm_res_ln_kernel,
        grid=(m // tm,),
        in_specs=[pl.BlockSpec((tm, k), lambda i: (i, 0)),
                  pl.BlockSpec((k, n), lambda i: (0, 0)),
                  pl.BlockSpec((tm, n), lambda i: (i, 0)),
                  pl.BlockSpec((1, n), lambda i: (0, 0)),
                  pl.BlockSpec((1, n), lambda i: (0, 0))],
        out_specs=pl.BlockSpec((tm, n), lambda i: (i, 0)),
        out_shape=jax.ShapeDtypeStruct((m, n), F32),
        compiler_params=_cparams(("parallel",)),
    )(a, w, res, g.reshape(1, n), b.reshape(1, n))


def _merge_kernel(a1_ref, a2_ref, w1_ref, w2_ref, ga_ref, gb_ref, o_ref):
    ya = _dot(a1_ref[...], w1_ref[...])
    yb = _dot(a2_ref[...], w2_ref[...])
    o_ref[...] = (jax.nn.sigmoid(ga_ref[...]) * ya + jax.nn.sigmoid(gb_ref[...]) * yb).astype(o_ref.dtype)


def _merge(a1, a2, w1, w2, p32, tm, tn):
    m, k = a1.shape
    n = w1.shape[1]
    ga_blk, gb_blk = C_GA // tn, C_GB // tn
    return pl.pallas_call(
        _merge_kernel,
        grid=(m // tm, n // tn),
        in_specs=[pl.BlockSpec((tm, k), lambda i, j: (i, 0)),
                  pl.BlockSpec((tm, k), lambda i, j: (i, 0)),
                  pl.BlockSpec((k, tn), lambda i, j: (0, j)),
                  pl.BlockSpec((k, tn), lambda i, j: (0, j)),
                  pl.BlockSpec((tm, tn), lambda i, j: (i, ga_blk + j)),
                  pl.BlockSpec((tm, tn), lambda i, j: (i, gb_blk + j))],
        out_specs=pl.BlockSpec((tm, tn), lambda i, j: (i, j)),
        out_shape=jax.ShapeDtypeStruct((m, n), BF16),
        compiler_params=_cparams(("parallel", "parallel")),
    )(a1, a2, w1, w2, p32, p32)


def _hgrn_kernel(q_ref, f_ref, i_ref, g_ref, lb_ref, ng_ref, o_ref, st_ref, *, hb, rb):
    c = HG_CHUNK

    @pl.when(pl.program_id(2) == 0)
    def _():
        st_ref[...] = jnp.zeros_like(st_ref)

    l0 = lb_ref[0:1, :]
    l1 = lb_ref[1:2, :]
    mx = jnp.maximum(l0, l1)
    e0 = jnp.exp(l0 - mx)
    lb = e0 / (e0 + jnp.exp(l1 - mx))

    f = lb + (1.0 - lb) * jax.nn.sigmoid(f_ref[...])
    kk = 1.0 - f
    gcum = jnp.log(f)
    pos = lax.broadcasted_iota(jnp.int32, gcum.shape, 0) & (c - 1)
    for s in (1, 2, 4, 8, 16, 32):
        gcum = gcum + jnp.where(pos >= s, pltpu.roll(gcum, s, axis=0), 0.0)

    tril = (lax.broadcasted_iota(jnp.int32, (c, c), 0) >= lax.broadcasted_iota(jnp.int32, (c, c), 1))
    ng = ng_ref[...]
    for ci in range(rb // c):
        rows = slice(ci * c, (ci + 1) * c)
        for h in range(hb):
            cols = slice(h * HG_DK, (h + 1) * HG_DK)
            gc = gcum[rows, cols]
            g_ref_row = gc[c // 2 - 1:c // 2, :]
            g_last = gc[c - 1:c, :]
            qc = q_ref[rows, cols]
            kc = kk[rows, cols]
            vc = i_ref[rows, cols].astype(BF16)
            qa = (qc * jnp.exp(gc - g_ref_row)).astype(BF16)
            kb = (kc * jnp.exp(g_ref_row - gc)).astype(BF16)
            a = jnp.where(tril, _dot_nt(qa, kb), 0.0)
            o = _dot(a.astype(BF16), vc)
            st = st_ref[h]
            qd = (qc * jnp.exp(gc)).astype(BF16)
            o = o + _dot_nt(qd, st.astype(BF16))
            kd = (kc * jnp.exp(g_last - gc)).astype(BF16)
            st_ref[h] = st * jnp.exp(g_last) + _dot_tn(vc, kd)
            o = o * lax.rsqrt(jnp.mean(o * o, axis=-1, keepdims=True) + RMS_EPS) * ng[:, cols]
            gate = g_ref[rows, cols]
            o_ref[rows, cols] = (o * (gate * jax.nn.sigmoid(gate))).astype(o_ref.dtype)


def _hgrn(p32, lb_logits, norm_g, batch, seq, hb=4, rb=512):
    t = batch * seq
    w = hb * HG_DK
    nr = seq // rb
    hq, hf, hi, hg = (cc // w for cc in (C_HQ, C_HF, C_HI, C_HG))
    row = lambda b, h, r: b * nr + r
    in_spec = lambda off: pl.BlockSpec((rb, w), lambda b, h, r: (row(b, h, r), off + h))
    return pl.pallas_call(
        functools.partial(_hgrn_kernel, hb=hb, rb=rb),
        grid=(batch, HG_HEADS // hb, nr),
        in_specs=[in_spec(hq), in_spec(hf), in_spec(hi), in_spec(hg),
                  pl.BlockSpec((2, w), lambda b, h, r: (0, h)),
                  pl.BlockSpec((1, w), lambda b, h, r: (0, h))],
        out_specs=pl.BlockSpec((rb, w), lambda b, h, r: (row(b, h, r), h)),
        out_shape=jax.ShapeDtypeStruct((t, HG_WIDTH), BF16),
        scratch_shapes=[pltpu.VMEM((hb, HG_DK, HG_DK), F32)],
        compiler_params=_cparams(("parallel", "parallel", "arbitrary")),
    )(p32, p32, p32, p32, lb_logits, norm_g.reshape(1, HG_WIDTH))


def _dsa_prep_kernel(ckv_ref, ik_ref, aq_ref, kvg_ref, ikg_ref, ikb_ref, wuk_ref,
                     ckvn_ref, ikn_ref, qlat_ref):
    ckv = ckv_ref[...]
    ckvn_ref[...] = (ckv * lax.rsqrt(jnp.mean(ckv * ckv, axis=-1, keepdims=True) + RMS_EPS)
                     * kvg_ref[...]).astype(ckvn_ref.dtype)
    ikn_ref[...] = _layer_norm_rows(ik_ref[...], ikg_ref[...], ikb_ref[...]).astype(ikn_ref.dtype)
    scale = AT_DH ** -0.5
    for h in range(AT_HEADS):
        qh = aq_ref[:, h * AT_DH:(h + 1) * AT_DH]
        qlat_ref[:, h * KV_RANK:(h + 1) * KV_RANK] = (_dot_nt(qh, wuk_ref[h]) * scale).astype(qlat_ref.dtype)


def _dsa_prep(p32, p16, kv_g, ik_g, ik_b, w_uk, tm):
    t = p32.shape[0]
    return pl.pallas_call(
        _dsa_prep_kernel,
        grid=(t // tm,),
        in_specs=[pl.BlockSpec((tm, KV_RANK), lambda i: (i, C_CKV // KV_RANK)),
                  pl.BlockSpec((tm, IDX_DIM), lambda i: (i, C_IK // IDX_DIM)),
                  pl.BlockSpec((tm, AT_WIDTH), lambda i: (i, 0)),
                  pl.BlockSpec((1, KV_RANK), lambda i: (0, 0)),
                  pl.BlockSpec((1, IDX_DIM), lambda i: (0, 0)),
                  pl.BlockSpec((1, IDX_DIM), lambda i: (0, 0)),
                  pl.BlockSpec((AT_HEADS, KV_RANK, AT_DH), lambda i: (0, 0, 0))],
        out_specs=[pl.BlockSpec((tm, KV_RANK), lambda i: (i, 0)),
                   pl.BlockSpec((tm, IDX_DIM), lambda i: (i, 0)),
                   pl.BlockSpec((tm, AT_HEADS * KV_RANK), lambda i: (i, 0))],
        out_shape=[jax.ShapeDtypeStruct((t, KV_RANK), BF16),
                   jax.ShapeDtypeStruct((t, IDX_DIM), BF16),
                   jax.ShapeDtypeStruct((t, AT_HEADS * KV_RANK), BF16)],
        compiler_params=_cparams(("parallel",)),
    )(p32, p32, p16, kv_g.reshape(1, -1), ik_g.reshape(1, -1), ik_b.reshape(1, -1), w_uk)


def _dsa_kernel(iq_ref, iwt_ref, ik_ref, qlat_ref, ckv_ref, ckvt_ref, wuvt_ref, near_ref,
                o_ref, key_ref, lg_ref, *, tq, seq, topk, tile_lo):
    i = tile_lo + pl.program_id(1)
    t0 = i * tq
    ik = ik_ref[...]
    w_scale = (IDX_DIM ** -0.5) * (IDX_HEADS ** -0.5)
    sc = jnp.zeros((seq, tq), F32)
    for h in range(IDX_HEADS):
        s = _dot_nt(ik, iq_ref[:, h * IDX_DIM:(h + 1) * IDX_DIM])
        sc = sc + jnp.maximum(s, 0.0) * (iwt_ref[h:h + 1, :] * w_scale)
    kpos = lax.broadcasted_iota(jnp.int32, (seq, tq), 0)
    tpos = t0 + lax.broadcasted_iota(jnp.int32, (seq, tq), 1)
    causal = kpos <= tpos
    sc = jnp.where(causal, sc + 0.0, NEG_INF)
    bits = pltpu.bitcast(sc, jnp.int32)
    key_ref[...] = bits ^ ((bits >> 31) & jnp.int32(0x7FFFFFFF))

    def bit_step(it, thr):
        cand = thr + jnp.left_shift(jnp.int32(1), 31 - it)
        cnt = jnp.sum((key_ref[...] >= cand).astype(F32), axis=0, keepdims=True)
        return jnp.where(cnt >= float(topk), cand, thr)

    thr = lax.fori_loop(0, 32, bit_step, jnp.full((1, tq), jnp.iinfo(jnp.int32).min, jnp.int32))
    addmask = jnp.where((key_ref[...] >= thr) & causal, 0.0, NEG_INF)

    wn = tq + REL_MAX_DIST
    for h in range(AT_HEADS):
        lg_h = lg_ref.at[h % 2]
        lg_h[...] = _dot_nt(ckv_ref[...], qlat_ref[:, h * KV_RANK:(h + 1) * KV_RANK]) + addmask
        if tile_lo == 0:
            lg_h[0:tq, :] += near_ref[h, REL_MAX_DIST:wn, :]
        else:
            start = pl.multiple_of(t0 - REL_MAX_DIST, REL_MAX_DIST)
            lg_h[pl.ds(start, wn), :] += near_ref[h]
        lg = lg_h[...]
        p = jnp.exp(lg - jnp.max(lg, axis=0, keepdims=True))
        denom = jnp.sum(p, axis=0, keepdims=True)
        olat = _dot(ckvt_ref[...], p.astype(BF16)) / denom
        out_t = _dot(wuvt_ref[h], olat.astype(BF16))
        o_ref[:, h * AT_DH:(h + 1) * AT_DH] = out_t.T.astype(o_ref.dtype)


def _near_bias_table(rel_bias, tq):
    wn = tq + REL_MAX_DIST
    d = (jnp.arange(tq, dtype=jnp.int32)[None, :] - jnp.arange(wn, dtype=jnp.int32)[:, None]
         + REL_MAX_DIST)
    max_exact = REL_BUCKETS // 2
    dd = jnp.maximum(d, 0)
    large = max_exact + (jnp.log(jnp.maximum(dd, 1).astype(F32) / max_exact)
                         / math.log(REL_MAX_DIST / max_exact) * (REL_BUCKETS - max_exact)).astype(jnp.int32)
    large = jnp.minimum(large, REL_BUCKETS - 1)
    bucket = jnp.where(dd < max_exact, dd, large)
    rb = rel_bias.astype(F32) - rel_bias.astype(F32)[REL_BUCKETS - 1]
    onehot = (bucket[None, :, :] == jnp.arange(REL_BUCKETS, dtype=jnp.int32)[:, None, None]).astype(F32)
    return jnp.einsum("bh,bcr->hcr", rb, onehot, precision=lax.Precision.HIGHEST)


def _dsa_group(p16, iw_t, ikn, qlat, ckvn, ckvn_t, w_uv_t, near, batch, seq, tq, tile_lo, n_tiles):
    nq = seq // tq
    klen = (tile_lo + n_tiles) * tq
    topk = min(TOPK_MAX, seq // 4)
    wn = tq + REL_MAX_DIST
    qrow = lambda b, i: b * nq + tile_lo + i
    return pl.pallas_call(
        functools.partial(_dsa_kernel, tq=tq, seq=klen, topk=topk, tile_lo=tile_lo),
        grid=(batch, n_tiles),
        in_specs=[pl.BlockSpec((tq, IDX_HEADS * IDX_DIM), lambda b, i: (qrow(b, i), 1)),
                  pl.BlockSpec((IDX_HEADS, tq), lambda b, i: (0, qrow(b, i))),
                  pl.BlockSpec((None, klen, IDX_DIM), lambda b, i: (b, 0, 0)),
                  pl.BlockSpec((tq, AT_HEADS * KV_RANK), lambda b, i: (qrow(b, i), 0)),
                  pl.BlockSpec((None, klen, KV_RANK), lambda b, i: (b, 0, 0)),
                  pl.BlockSpec((None, KV_RANK, klen), lambda b, i: (b, 0, 0)),
                  pl.BlockSpec((AT_HEADS, AT_DH, KV_RANK), lambda b, i: (0, 0, 0)),
                  pl.BlockSpec((AT_HEADS, wn, tq), lambda b, i: (0, 0, 0))],
        out_specs=pl.BlockSpec((None, tq, AT_WIDTH), lambda b, i: (b, i, 0)),
        out_shape=jax.ShapeDtypeStruct((batch, n_tiles * tq, AT_WIDTH), BF16),
        scratch_shapes=[pltpu.VMEM((klen, tq), jnp.int32), pltpu.VMEM((2, klen, tq), F32)],
        compiler_params=_cparams(("parallel", "arbitrary")),
    )(p16, iw_t, ikn.reshape(batch, seq, IDX_DIM), qlat, ckvn.reshape(batch, seq, KV_RANK),
      ckvn_t, w_uv_t, near)


def _dsa(p16, iw_t, ikn, qlat, ckvn, ckvn_t, w_uv_t, near, batch, seq, tq):
    nq = seq // tq
    bounds = [0, 1] + list(range(2, nq + 1, 2))
    if bounds[-1] != nq:
        bounds.append(nq)
    outs = [_dsa_group(p16, iw_t, ikn, qlat, ckvn, ckvn_t, w_uv_t, near, batch, seq, tq, lo, hi - lo)
            for lo, hi in zip(bounds[:-1], bounds[1:])]
    return jnp.concatenate(outs, axis=1).reshape(batch * seq, AT_WIDTH)


def _xattn_kernel(q_ref, k_ref, v_ref, o_ref):
    scale = MEM_DH ** -0.5
    for h in range(MEM_HEADS):
        cols = slice(h * MEM_DH, (h + 1) * MEM_DH)
        lg = _dot_nt(q_ref[:, cols], k_ref[:, cols]) * scale
        p = jnp.exp(lg - jnp.max(lg, axis=-1, keepdims=True))
        p = p / jnp.sum(p, axis=-1, keepdims=True)
        o_ref[:, cols] = _dot(p.astype(BF16), v_ref[:, cols]).astype(o_ref.dtype)


def _xattn(q, k, v, batch, seq, mlen, tq):
    t = batch * seq
    nq = seq // tq
    return pl.pallas_call(
        _xattn_kernel,
        grid=(batch, nq),
        in_specs=[pl.BlockSpec((tq, D_MODEL), lambda b, i: (b * nq + i, 0)),
                  pl.BlockSpec((mlen, D_MODEL), lambda b, i: (b, 0)),
                  pl.BlockSpec((mlen, D_MODEL), lambda b, i: (b, 0))],
        out_specs=pl.BlockSpec((tq, D_MODEL), lambda b, i: (b * nq + i, 0)),
        out_shape=jax.ShapeDtypeStruct((t, D_MODEL), BF16),
        compiler_params=_cparams(("parallel", "parallel")),
    )(q, k, v)


def _top_values(x, k):
    rows = lax.broadcasted_iota(jnp.int32, (k, x.shape[1]), 0)
    out = jnp.zeros((k, x.shape[1]), F32)
    for a in range(k):
        mx = jnp.max(x, axis=0, keepdims=True)
        out = jnp.where(rows == a, mx, out)
        x = jnp.where(x == mx, NEG_INF, x)
    return out


def _peer_kernel(qp_ref, keys_ref, xt_ref, u_ref, vt_ref, res_ref, g_ref, b_ref, o_ref,
                 s_ref, e_ref, acc_ref, wg_ref, *, tt, te, tc):
    e = pl.program_id(1)
    nk = PEER_NKEYS
    k = PEER_TOPK

    @pl.when(e == 0)
    def _():
        acc_ref[...] = jnp.zeros_like(acc_ref)
        for hp in range(2 * PEER_HEADS):
            s_ref[hp] = _dot_nt(keys_ref[hp], qp_ref[:, hp * nk:(hp + 1) * nk])
        for h in range(PEER_HEADS):
            s0 = s_ref[2 * h]
            s1 = s_ref[2 * h + 1]
            v0 = _top_values(s0, k)
            v1 = _top_values(s1, k)
            half = k // 2
            cand = jnp.concatenate(
                [v0[0:1, :] + v1]
                + [v0[a:a + 1, :] + v1[0:half, :] for a in range(1, half)]
                + [v0[half:k, :] + v1[0:1, :]], axis=0)
            top = _top_values(cand, k)
            z = jnp.sum(jnp.exp(top - top[0:1, :]), axis=0, keepdims=True)
            theta = top[k - 1:k, :]
            e_ref[2 * h] = jnp.exp(s0 - v0[0:1, :]) * (0.5 / z)
            e_ref[2 * h + 1] = jnp.exp(s1 - v1[0:1, :])
            cut = jnp.full(s0.shape, jnp.inf, F32)
            for a in range(k):
                v0a = v0[a:a + 1, :]
                cut_a = jnp.min(jnp.where(v0a + v1 >= theta, v1, jnp.inf), axis=0, keepdims=True)
                cut = jnp.where(s0 == v0a, cut_a, cut)
            s_ref[2 * h] = cut

    hts = [_dot(u_ref[c * tc:(c + 1) * tc, :], xt_ref[...]) for c in range(te // tc)]
    acc = acc_ref[...]
    for c in range(te // tc):
        rows = slice(c * tc, (c + 1) * tc)
        ht = hts[c]
        gel = ht * (1.0 + lax.erf(ht * (2.0 ** -0.5)))
        for ii in range(tc // nk):
            irow = e * (te // nk) + c * (tc // nk) + ii
            sub = slice(ii * nk, (ii + 1) * nk)
            cutrows = [s_ref[2 * h, pl.ds(irow, 1), :] for h in range(PEER_HEADS)]
            e0rows = [e_ref[2 * h, pl.ds(irow, 1), :] for h in range(PEER_HEADS)]
            for lc in range(tt // LANES):
                tok = slice(lc * LANES, (lc + 1) * LANES)
                w = None
                for h in range(PEER_HEADS):
                    wh = jnp.where(s_ref[2 * h + 1, :, tok] >= cutrows[h][:, tok],
                                   e0rows[h][:, tok] * e_ref[2 * h + 1, :, tok], 0.0)
                    w = wh if w is None else w + wh
                wg_ref[c * tc + ii * nk:c * tc + (ii + 1) * nk, tok] = (w * gel[sub, tok]).astype(BF16)
        acc = acc + _dot(vt_ref[:, rows], wg_ref[rows, :])
    acc_ref[...] = acc

    @pl.when(e == pl.num_programs(1) - 1)
    def _():
        for c in range(tt // tc):
            tok = slice(c * tc, (c + 1) * tc)
            y = ALPHA * res_ref[tok, :] + acc_ref[:, tok].T
            o_ref[tok, :] = _layer_norm_rows(y, g_ref[...], b_ref[...])


def _peer(qp, keys, h2, u, v_t, g, b, tt, te, tc):
    t, d = h2.shape
    n = u.shape[0]
    return pl.pallas_call(
        functools.partial(_peer_kernel, tt=tt, te=te, tc=tc),
        grid=(t // tt, n // te),
        in_specs=[pl.BlockSpec((tt, PEER_HEADS * PEER_DKEY), lambda i, e: (i, 0)),
                  pl.BlockSpec((2 * PEER_HEADS, PEER_NKEYS, PEER_DKEY // 2), lambda i, e: (0, 0, 0)),
                  pl.BlockSpec((d, tt), lambda i, e: (0, i)),
                  pl.BlockSpec((te, d), lambda i, e: (e, 0)),
                  pl.BlockSpec((d, te), lambda i, e: (0, e)),
                  pl.BlockSpec((tt, d), lambda i, e: (i, 0)),
                  pl.BlockSpec((1, d), lambda i, e: (0, 0)),
                  pl.BlockSpec((1, d), lambda i, e: (0, 0))],
        out_specs=pl.BlockSpec((tt, d), lambda i, e: (i, 0)),
        out_shape=jax.ShapeDtypeStruct((t, d), F32),
        scratch_shapes=[pltpu.VMEM((2 * PEER_HEADS, PEER_NKEYS, tt), F32),
                        pltpu.VMEM((2 * PEER_HEADS, PEER_NKEYS, tt), F32),
                        pltpu.VMEM((d, tt), F32),
                        pltpu.VMEM((te, tt), BF16)],
        compiler_params=_cparams(("parallel", "arbitrary")),
    )(qp, keys, h2.astype(BF16).T, u, v_t, h2, g.reshape(1, d), b.reshape(1, d))


def kernel(x, mem, w_in, lb_logits, hg_norm_g, kv_norm_g, ik_norm_g, ik_norm_b, w_uk, w_uv,
           rel_bias, w_branch_a, w_branch_b, w_mix_out, ln1_g, ln1_b, w_cq, w_ck, w_cv, w_co,
           ln2_g, ln2_b, w_pq, peer_keys, peer_u, peer_v, ln3_g, ln3_b):
    batch, seq, d = x.shape
    mlen = mem.shape[1]
    t = batch * seq
    l = 0

    wi = w_in[l]
    o_aq = 4 * HG_WIDTH
    o_ckv = o_aq + AT_WIDTH
    o_iq = o_ckv + KV_RANK
    o_ik = o_iq + IDX_HEADS * IDX_DIM
    o_iw = o_ik + IDX_DIM
    o_ga = o_iw + IDX_HEADS
    w32 = jnp.concatenate([wi[:, :o_aq], wi[:, o_ckv:o_iq], wi[:, o_ik:o_iw],
                           jnp.pad(wi[:, o_iw:o_ga], ((0, 0), (0, LANES - IDX_HEADS))),
                           wi[:, o_ga:]], axis=1).astype(BF16)
    w16 = jnp.concatenate([wi[:, o_aq:o_ckv], wi[:, o_iq:o_ik]], axis=1).astype(BF16)
    xf = x.reshape(t, d)
    xb = xf.astype(BF16)

    p32 = _matmul(xb, w32, 2048, 512, F32)
    p16 = _matmul(xb, w16, 2048, 1024, BF16)
    ya_in = _hgrn(p32, lb_logits, hg_norm_g[l], batch, seq)
    ckvn, ikn, qlat = _dsa_prep(p32, p16, kv_norm_g[l], ik_norm_g[l], ik_norm_b[l],
                                w_uk[l].astype(BF16), 1024)
    tq = 256
    iw_t = p32[:, C_IW:C_IW + IDX_HEADS].T
    ckvn_t = jnp.swapaxes(ckvn.reshape(batch, seq, KV_RANK), 1, 2)
    w_uv_t = jnp.swapaxes(w_uv[l], 1, 2).astype(BF16)
    near = _near_bias_table(rel_bias, tq)
    yb_in = _dsa(p16, iw_t, ikn, qlat, ckvn, ckvn_t, w_uv_t, near, batch, seq, tq)
    mix_in = _merge(ya_in, yb_in, w_branch_a[l].astype(BF16), w_branch_b[l].astype(BF16), p32, 1024, 512)
    h1 = _matmul_res_ln(mix_in, w_mix_out[l].astype(BF16), xf, ln1_g[l], ln1_b[l], 512)

    qc = _matmul(h1, w_cq[l].astype(BF16), 1024, 1024, BF16)
    memf = mem.reshape(batch * mlen, d).astype(BF16)
    kc = _matmul(memf, w_ck[l].astype(BF16), 1024, 1024, BF16)
    vc = _matmul(memf, w_cv[l].astype(BF16), 1024, 1024, BF16)
    ca_in = _xattn(qc, kc, vc, batch, seq, mlen, 512)
    h2 = _matmul_res_ln(ca_in, w_co[l].astype(BF16), h1, ln2_g[l], ln2_b[l], 512)

    qp = _matmul(h2, w_pq[l].astype(BF16), 1024, 1024, BF16)
    keys = peer_keys[l].reshape(2 * PEER_HEADS, PEER_NKEYS, PEER_DKEY // 2).astype(BF16)
    u = peer_u[l].astype(BF16)
    v_t = peer_v[l].astype(BF16).T
    h3 = _peer(qp, keys, h2, u, v_t, ln3_g[l], ln3_b[l], 512, 512, 512)
    return h3.reshape(batch, seq, d)
```
